```python
import math
import jax, jax.numpy as jnp
from jax import lax
import numpy as np

D_MODEL = 1024
BATCH = 32
SEQ = 2048
DEPTH = 1
DEC_BATCH = 2
DEC_SEQ = 8192
PAST_LEN = 128

SSD_D_INNER = 1024
SSD_HEAD_DIM = 64
SSD_N_HEADS = SSD_D_INNER // SSD_HEAD_DIM
SSD_N_GROUPS = 2
SSD_D_STATE = 64
SSD_CONV = 5
SSD_CHUNK = 128
SSD_CONV_DIM = SSD_D_INNER + 2 * SSD_N_GROUPS * SSD_D_STATE
ATTN_HEADS = 8
ATTN_HEAD_DIM = 64
ATTN_V_DIM = 2 * ATTN_HEAD_DIM
ATTN_QK_WIDTH = ATTN_HEADS * 2 * ATTN_HEAD_DIM
ATTN_V_WIDTH = ATTN_HEADS * ATTN_V_DIM
Q_BLOCK = 128
REL_BUCKETS = 32
REL_MAX_DIST = 128
PEER_N_KEYS = 128
PEER_N_EXPERTS = PEER_N_KEYS * PEER_N_KEYS
PEER_HEADS = 8
PEER_D_KEY = 128
PEER_TOPK = 16
PEER_TOKEN_BLOCK = 128
IN_WIDTH = SSD_D_INNER + SSD_CONV_DIM + 2 * SSD_N_HEADS + 2 * ATTN_QK_WIDTH + ATTN_V_WIDTH + 2 * D_MODEL
EPS = 1e-6

kernel_name = "hybrid_ssd_diffattn_peer_encoder"


def rmsnorm(x, g):
    x32 = x.astype(jnp.float32)
    y = x32 * lax.rsqrt(jnp.mean(x32 * x32, axis=-1, keepdims=True) + EPS)
    return (y * g).astype(x.dtype)


def rel_bucket(rel):
    nb = REL_BUCKETS // 2
    ret = jnp.where(rel > 0, nb, 0)
    n = jnp.abs(rel)
    max_exact = nb // 2
    nf = jnp.maximum(n, 1).astype(jnp.float32)
    large = max_exact + (jnp.log(nf / max_exact) / math.log(REL_MAX_DIST / max_exact)
                         * (nb - max_exact)).astype(jnp.int32)
    large = jnp.minimum(large, nb - 1)
    return ret + jnp.where(n < max_exact, n, large)


def depthwise_conv_centred(x, w, b):
    pad = (SSD_CONV - 1) // 2
    y = lax.conv_general_dilated(x, w.astype(x.dtype), window_strides=(1,), padding=[(pad, pad)],
                                 dimension_numbers=("NWC", "WIO", "NWC"),
                                 feature_group_count=x.shape[-1])
    return y + b


def ssd_chunked(x, dt, A, Bm, Cm):
    b, l, nh, p = x.shape
    g, n = Bm.shape[2], Bm.shape[3]
    hg = nh // g
    L = SSD_CHUNK
    c = l // L
    X = (x.astype(jnp.float32) * dt[..., None]).reshape(b, c, L, g, hg, p)
    a = (dt * A).reshape(b, c, L, g, hg).transpose(0, 3, 4, 1, 2)
    a_cs = jnp.cumsum(a, axis=-1)
    Bc = Bm.astype(jnp.float32).reshape(b, c, L, g, n)
    Cc = Cm.astype(jnp.float32).reshape(b, c, L, g, n)
    mask = jnp.tril(jnp.ones((L, L), dtype=bool))
    seg = a_cs[..., :, None] - a_cs[..., None, :]
    decay = jnp.exp(jnp.where(mask, seg, -jnp.inf))
    cb = jnp.einsum("bclgn,bcsgn->bgcls", Cc, Bc)
    m = cb[:, :, None] * decay
    y_diag = jnp.einsum("bghcls,bcsghp->bclghp", m, X)
    decay_states = jnp.exp(a_cs[..., -1:] - a_cs)
    states = jnp.einsum("bclgn,bghcl,bclghp->bcghpn", Bc, decay_states, X)
    chunk_decay = jnp.exp(a_cs[..., -1])

    def step(carry, inp):
        st, dec = inp
        return carry * dec[..., None, None] + st, carry

    init = jnp.zeros((b, g, hg, p, n), jnp.float32)
    _, prev = lax.scan(step, init, (states.transpose(1, 0, 2, 3, 4, 5), chunk_decay.transpose(3, 0, 1, 2)))
    y_off = jnp.einsum("bclgn,cbghpn,bghcl->bclghp", Cc, prev, jnp.exp(a_cs))
    return (y_diag + y_off).reshape(b, l, nh, p)


def ssd_mixer(z, xBC, dt_f_raw, dt_b_raw, conv_w, conv_b, dt_bias_f, dt_bias_b,
              a_log_f, a_log_b, d_skip, norm_gain):
    b, l, _ = xBC.shape
    xBC = jax.nn.silu(depthwise_conv_centred(xBC, conv_w, conv_b))
    xs, Bm, Cm = jnp.split(xBC, [SSD_D_INNER, SSD_D_INNER + SSD_N_GROUPS * SSD_D_STATE], axis=-1)
    xs = xs.reshape(b, l, SSD_N_HEADS, SSD_HEAD_DIM)
    Bm = Bm.reshape(b, l, SSD_N_GROUPS, SSD_D_STATE)
    Cm = Cm.reshape(b, l, SSD_N_GROUPS, SSD_D_STATE)

    def direction(dt_raw, dt_bias, a_log, flip):
        dt = jax.nn.softplus(dt_raw.astype(jnp.float32) + dt_bias.astype(jnp.float32))
        A = -jnp.exp(a_log.astype(jnp.float32))
        if flip:
            y = ssd_chunked(xs[:, ::-1], dt[:, ::-1], A, Bm[:, ::-1], Cm[:, ::-1])
            return y[:, ::-1]
        return ssd_chunked(xs, dt, A, Bm, Cm)

    y = (direction(dt_f_raw, dt_bias_f, a_log_f, False)
         + direction(dt_b_raw, dt_bias_b, a_log_b, True)
         + xs.astype(jnp.float32) * d_skip.astype(jnp.float32)[:, None])
    y = y.reshape(b, l, SSD_D_INNER) * jax.nn.silu(z.astype(jnp.float32))
    yg = y.reshape(b, l, SSD_N_GROUPS, SSD_D_INNER // SSD_N_GROUPS)
    yg = yg * lax.rsqrt(jnp.mean(yg * yg, axis=-1, keepdims=True) + EPS)
    y = (yg * norm_gain.reshape(SSD_N_GROUPS, -1)).reshape(b, l, SSD_D_INNER)
    return y.astype(z.dtype)


def diff_attention(q, k, v, q_gain, k_gain, lam_q1, lam_k1, lam_q2, lam_k2, subln_gain, rel_table, lam_init):
    B, S = q.shape[0], q.shape[1]
    q = rmsnorm(q, q_gain) * (ATTN_HEAD_DIM ** -0.5)
    k = rmsnorm(k, k_gain)
    f32 = jnp.float32
    lam = (jnp.exp(jnp.sum(lam_q1.astype(f32) * lam_k1.astype(f32)))
           - jnp.exp(jnp.sum(lam_q2.astype(f32) * lam_k2.astype(f32))) + lam_init)
    kh = k.transpose(0, 2, 3, 1, 4)
    vh = v.transpose(0, 2, 1, 3)
    nb = S // Q_BLOCK
    qb = q.reshape(B, nb, Q_BLOCK, ATTN_HEADS, 2, ATTN_HEAD_DIM).transpose(1, 0, 3, 4, 2, 5)
    kpos = jnp.arange(S, dtype=jnp.int32)

    def block(args):
        qblk, start = args
        qpos = start + jnp.arange(Q_BLOCK, dtype=jnp.int32)
        bucket = rel_bucket(kpos[None, :] - qpos[:, None])
        bias = rel_table[bucket].astype(f32).transpose(2, 0, 1)
        s = jnp.einsum("bhmqd,bhmkd->bhmqk", qblk, kh).astype(f32) + bias[None, :, None]
        p = jax.nn.softmax(s, axis=-1)
        w = p[:, :, 0] - lam * p[:, :, 1]
        return jnp.einsum("bhqk,bhkd->bhqd", w.astype(vh.dtype), vh)

    o = lax.map(block, (qb, jnp.arange(nb, dtype=jnp.int32) * Q_BLOCK))
    o = rmsnorm(o, subln_gain) * (1.0 - lam_init)
    return o.transpose(1, 0, 3, 2, 4).reshape(B, S, ATTN_V_WIDTH)


def peer(x, w_q, sub_keys, u, v):
    xf = x.reshape(-1, D_MODEL)
    T = xf.shape[0]
    tb = PEER_TOKEN_BLOCK
    K = PEER_TOPK

    def block(xb):
        q = (xb @ w_q).reshape(tb, PEER_HEADS, 2, PEER_D_KEY // 2)
        s = jnp.einsum("thmd,mkd->thmk", q, sub_keys).astype(jnp.float32)
        s_top, i_top = lax.top_k(s, K)
        cand = (s_top[:, :, 0, :, None] + s_top[:, :, 1, None, :]).reshape(tb, PEER_HEADS, K * K)
        cand_idx = (i_top[:, :, 0, :, None] * PEER_N_KEYS + i_top[:, :, 1, None, :]).reshape(tb, PEER_HEADS, K * K)
        best, pos = lax.top_k(cand, K)
        expert = jnp.take_along_axis(cand_idx, pos, axis=-1)
        gate = jax.nn.softmax(best, axis=-1)
        ue = u[expert]
        h = jax.nn.gelu(jnp.einsum("thkd,td->thk", ue, xb).astype(jnp.float32), approximate=False)
        w = (gate * h).astype(xb.dtype)
        ve = v[expert]
        return jnp.einsum("thk,thkd->td", w, ve)

    out = lax.map(block, xf.reshape(T // tb, tb, D_MODEL))
    return out.reshape(x.shape)


def encoder_layer(x, layer_idx, rel_bias, norm1_gain, w_in, conv_w, conv_b, dt_bias_f, dt_bias_b,
                  a_log_f, a_log_b, d_skip, ssd_norm_gain, w_ssd_proj, q_norm_gain, k_norm_gain,
                  lambda_q1, lambda_k1, lambda_q2, lambda_k2, subln_gain, w_attn_proj, w_out,
                  norm2_gain, peer_w_q, peer_sub_keys, peer_u, peer_v):
    B, S, _ = x.shape
    lam_init = 0.8 - 0.6 * math.exp(-0.3 * layer_idx)
    xn = rmsnorm(x, norm1_gain)
    proj = xn @ w_in
    o1 = SSD_D_INNER
    o2 = o1 + SSD_CONV_DIM
    o3 = o2 + SSD_N_HEADS
    o4 = o3 + SSD_N_HEADS
    o5 = o4 + ATTN_QK_WIDTH
    o6 = o5 + ATTN_QK_WIDTH
    o7 = o6 + ATTN_V_WIDTH
    o8 = o7 + D_MODEL
    z, xBC, dtf, dtb, q, k, v, g_ssd, g_attn = jnp.split(proj, [o1, o2, o3, o4, o5, o6, o7, o8], axis=-1)
    y_ssd = ssd_mixer(z, xBC, dtf, dtb, conv_w, conv_b, dt_bias_f, dt_bias_b,
                      a_log_f, a_log_b, d_skip, ssd_norm_gain) @ w_ssd_proj
    q = q.reshape(B, S, ATTN_HEADS, 2, ATTN_HEAD_DIM)
    k = k.reshape(B, S, ATTN_HEADS, 2, ATTN_HEAD_DIM)
    v = v.reshape(B, S, ATTN_HEADS, ATTN_V_DIM)
    y_attn = diff_attention(q, k, v, q_norm_gain, k_norm_gain, lambda_q1, lambda_k1, lambda_q2, lambda_k2,
                            subln_gain, rel_bias, lam_init) @ w_attn_proj
    mixed = jax.nn.sigmoid(g_ssd) * y_ssd + jax.nn.sigmoid(g_attn) * y_attn
    h = x + mixed @ w_out
    return h + peer(rmsnorm(h, norm2_gain), peer_w_q, peer_sub_keys, peer_u, peer_v)


def setup_inputs(seed: int = 0) -> dict:
    key = jax.random.key(seed)
    ks = jax.random.split(key, 32)
    f32 = jnp.float32
    nrm = lambda k, shape, s: jax.random.normal(k, shape, f32) * s
    gain = lambda k, shape: 1.0 + 0.05 * jax.random.normal(k, shape, f32)
    Lr = DEPTH
    dt_f = jnp.exp(jax.random.uniform(ks[6], (Lr, SSD_N_HEADS), f32, math.log(1e-3), math.log(1e-1)))
    dt_b = jnp.exp(jax.random.uniform(ks[7], (Lr, SSD_N_HEADS), f32, math.log(1e-3), math.log(1e-1)))
    inv_softplus = lambda d: d + jnp.log(-jnp.expm1(-d))
    return {
        "x_prompt": nrm(ks[0], (BATCH, SEQ, D_MODEL), 1.0),
        "x_sample": nrm(ks[1], (DEC_BATCH, DEC_SEQ, D_MODEL), 1.0),
        "rel_bias": nrm(ks[2], (REL_BUCKETS, ATTN_HEADS), 0.5),
        "norm1_gain": gain(ks[3], (Lr, D_MODEL)),
        "w_in": nrm(ks[4], (Lr, D_MODEL, IN_WIDTH), D_MODEL ** -0.5),
        "conv_w": nrm(ks[5], (Lr, SSD_CONV, 1, SSD_CONV_DIM), SSD_CONV ** -0.5),
        "conv_b": nrm(ks[8], (Lr, SSD_CONV_DIM), 0.02),
        "dt_bias_f": inv_softplus(dt_f),
        "dt_bias_b": inv_softplus(dt_b),
        "a_log_f": jnp.log(jax.random.uniform(ks[9], (Lr, SSD_N_HEADS), f32, 1.0, 16.0)),
        "a_log_b": jnp.log(jax.random.uniform(ks[10], (Lr, SSD_N_HEADS), f32, 1.0, 16.0)),
        "d_skip": gain(ks[11], (Lr, SSD_N_HEADS)),
        "ssd_norm_gain": gain(ks[12], (Lr, SSD_D_INNER)),
        "w_ssd_proj": nrm(ks[13], (Lr, SSD_D_INNER, D_MODEL), SSD_D_INNER ** -0.5),
        "q_norm_gain": gain(ks[14], (Lr, ATTN_HEAD_DIM)),
        "k_norm_gain": gain(ks[15], (Lr, ATTN_HEAD_DIM)),
        "lambda_q1": nrm(ks[16], (Lr, ATTN_HEAD_DIM), 0.1),
        "lambda_k1": nrm(ks[17], (Lr, ATTN_HEAD_DIM), 0.1),
        "lambda_q2": nrm(ks[18], (Lr, ATTN_HEAD_DIM), 0.1),
        "lambda_k2": nrm(ks[19], (Lr, ATTN_HEAD_DIM), 0.1),
        "subln_gain": gain(ks[20], (Lr, ATTN_V_DIM)),
        "w_attn_proj": nrm(ks[21], (Lr, ATTN_V_WIDTH, D_MODEL), ATTN_V_WIDTH ** -0.5),
        "w_out": nrm(ks[22], (Lr, D_MODEL, D_MODEL), D_MODEL ** -0.5),
        "norm2_gain": gain(ks[23], (Lr, D_MODEL)),
        "peer_w_q": nrm(ks[24], (Lr, D_MODEL, PEER_HEADS * PEER_D_KEY), D_MODEL ** -0.5),
        "peer_sub_keys": nrm(ks[25], (Lr, 2, PEER_N_KEYS, PEER_D_KEY // 2), (PEER_D_KEY // 2) ** -0.5),
        "peer_u": nrm(ks[26], (Lr, PEER_N_EXPERTS, D_MODEL), D_MODEL ** -0.5),
        "peer_v": nrm(ks[27], (Lr, PEER_N_EXPERTS, D_MODEL), PEER_HEADS ** -0.5),
    }


def _trunk(x, rel_bias, norm1_gain, w_in, conv_w, conv_b, dt_bias_f, dt_bias_b, a_log_f, a_log_b,
           d_skip, ssd_norm_gain, w_ssd_proj, q_norm_gain, k_norm_gain, lambda_q1, lambda_k1,
           lambda_q2, lambda_k2, subln_gain, w_attn_proj, w_out, norm2_gain, peer_w_q,
           peer_sub_keys, peer_u, peer_v):
    for i in range(DEPTH):
        x = encoder_layer(x, i, rel_bias, norm1_gain[i], w_in[i], conv_w[i], conv_b[i], dt_bias_f[i],
                          dt_bias_b[i], a_log_f[i], a_log_b[i], d_skip[i], ssd_norm_gain[i], w_ssd_proj[i],
                          q_norm_gain[i], k_norm_gain[i], lambda_q1[i], lambda_k1[i], lambda_q2[i],
                          lambda_k2[i], subln_gain[i], w_attn_proj[i], w_out[i], norm2_gain[i],
                          peer_w_q[i], peer_sub_keys[i], peer_u[i], peer_v[i])
    return x


def reference(x_prompt, x_sample, rel_bias, norm1_gain, w_in, conv_w, conv_b, dt_bias_f, dt_bias_b,
              a_log_f, a_log_b, d_skip, ssd_norm_gain, w_ssd_proj, q_norm_gain, k_norm_gain,
              lambda_q1, lambda_k1, lambda_q2, lambda_k2, subln_gain, w_attn_proj, w_out,
              norm2_gain, peer_w_q, peer_sub_keys, peer_u, peer_v):
    y_prompt = _trunk(x_prompt, rel_bias, norm1_gain, w_in, conv_w, conv_b, dt_bias_f, dt_bias_b,
                      a_log_f, a_log_b, d_skip, ssd_norm_gain, w_ssd_proj, q_norm_gain, k_norm_gain,
                      lambda_q1, lambda_k1, lambda_q2, lambda_k2, subln_gain, w_attn_proj, w_out,
                      norm2_gain, peer_w_q, peer_sub_keys, peer_u, peer_v)
    y_sample = _trunk(x_sample, rel_bias, norm1_gain, w_in, conv_w, conv_b, dt_bias_f, dt_bias_b,
                      a_log_f, a_log_b, d_skip, ssd_norm_gain, w_ssd_proj, q_norm_gain, k_norm_gain,
                      lambda_q1, lambda_k1, lambda_q2, lambda_k2, subln_gain, w_attn_proj, w_out,
                      norm2_gain, peer_w_q, peer_sub_keys, peer_u, peer_v)
    return (y_prompt, y_sample)
```

```python
import functools
import math

import jax
import jax.numpy as jnp
from jax import lax
from jax.experimental import pallas as pl
from jax.experimental.pallas import tpu as pltpu

F32 = jnp.float32
BF16 = jnp.bfloat16

D_MODEL = 1024
SSD_D_INNER = 1024
SSD_HEAD_DIM = 64
SSD_N_HEADS = 16
SSD_N_GROUPS = 2
SSD_D_STATE = 64
SSD_CONV = 5
SSD_CHUNK = 128
SSD_CONV_DIM = 1280
ATTN_HEADS = 8
ATTN_HEAD_DIM = 64
ATTN_V_DIM = 128
REL_BUCKETS = 32
REL_MAX_DIST = 128
PEER_N_KEYS = 128
PEER_HEADS = 8
PEER_TOPK = 16
EPS = 1e-6

LANES = 128
SUBLANES = 8
VMEM_LIMIT = 56 * 1024 * 1024

COL_Z = 0
COL_Q = 1024
COL_K = 2048
COL_V = 3072
COL_GS = 4096
COL_GA = 5120
COL_XBC = 6400
PROJ_W = 7680
PROJ_TN = 1536

_CAND = [(a, b) for a in range(PEER_TOPK) for b in range(PEER_TOPK) if (a + 1) * (b + 1) <= PEER_TOPK]
_CAND_ROWS = 56


def _dot(a, b):
    return jnp.dot(a, b, preferred_element_type=F32)


def _dot_nt(a, b):
    return lax.dot_general(a, b, (((1,), (1,)), ((), ())), preferred_element_type=F32)


def _split3(v):
    hi = v.astype(BF16)
    r = v - hi.astype(F32)
    mid = r.astype(BF16)
    lo = (r - mid.astype(F32)).astype(BF16)
    return hi, mid, lo


def _exact_dot_rhs(v, m):
    hi, mid, lo = _split3(v)
    return _dot(hi, m) + _dot(mid, m) + _dot(lo, m)


def _exact_dot_lhs(m, v):
    hi, mid, lo = _split3(v)
    return _dot(m, hi) + _dot(m, mid) + _dot(m, lo)


def _cparams(sem):
    return pltpu.CompilerParams(dimension_semantics=sem, vmem_limit_bytes=VMEM_LIMIT)


def _inproj_kernel(x_ref, g_ref, w_ref, wdt_ref, o_ref, dt_ref, xn_ref):
    j = pl.program_id(1)

    @pl.when(j == 0)
    def _():
        x = x_ref[...]
        ms = jnp.mean(x * x, axis=-1, keepdims=True)
        xn = (x * lax.rsqrt(ms + EPS) * g_ref[...]).astype(BF16)
        xn_ref[...] = xn
        dt_ref[...] = _dot(xn, wdt_ref[...])

    o_ref[...] = _dot(xn_ref[...], w_ref[...]).astype(BF16)


def _inproj(x2, gain, w_slab, w_dt, tm):
    T = x2.shape[0]
    grid = (T // tm, PROJ_W // PROJ_TN)
    return pl.pallas_call(
        _inproj_kernel,
        grid=grid,
        in_specs=[
            pl.BlockSpec((tm, D_MODEL), lambda i, j: (i, 0)),
            pl.BlockSpec((1, D_MODEL), lambda i, j: (0, 0)),
            pl.BlockSpec((D_MODEL, PROJ_TN), lambda i, j: (0, j)),
            pl.BlockSpec((D_MODEL, LANES), lambda i, j: (0, 0)),
        ],
        out_specs=[
            pl.BlockSpec((tm, PROJ_TN), lambda i, j: (i, j)),
            pl.BlockSpec((tm, LANES), lambda i, j: (i, 0)),
        ],
        out_shape=[
            jax.ShapeDtypeStruct((T, PROJ_W), BF16),
            jax.ShapeDtypeStruct((T, LANES), F32),
        ],
        scratch_shapes=[pltpu.VMEM((tm, D_MODEL), BF16)],
        compiler_params=_cparams(("parallel", "arbitrary")),
        name="inproj",
    )(x2, gain, w_slab, w_dt)


def _conv_kernel(xp_ref, xc_ref, xn_ref, cw_ref, cb_ref, o_ref, pad_ref, *, nc):
    c = pl.program_id(1)
    L = SSD_CHUNK
    H = SUBLANES
    xp = jnp.where(c > 0, xp_ref[...].astype(F32), 0.0)
    xn = jnp.where(c < nc - 1, xn_ref[...].astype(F32), 0.0)
    pad_ref[0:H, :] = xp
    pad_ref[H:H + L, :] = xc_ref[...].astype(F32)
    pad_ref[H + L:2 * H + L, :] = xn
    half = (SSD_CONV - 1) // 2
    acc = jnp.broadcast_to(cb_ref[...], (L, SSD_CONV_DIM))
    for k in range(SSD_CONV):
        off = H - half + k
        acc = acc + cw_ref[k:k + 1, :] * pad_ref[off:off + L, :]
    o_ref[...] = (acc * jax.nn.sigmoid(acc)).astype(BF16)


def _conv(proj3, cw, cb):
    B, S, _ = proj3.shape
    L = SSD_CHUNK
    nc = S // L
    rb = L // SUBLANES
    cblk = COL_XBC // SSD_CONV_DIM
    return pl.pallas_call(
        functools.partial(_conv_kernel, nc=nc),
        grid=(B, nc),
        in_specs=[
            pl.BlockSpec((None, SUBLANES, SSD_CONV_DIM), lambda b, c: (b, jnp.maximum(c * rb - 1, 0), cblk)),
            pl.BlockSpec((None, L, SSD_CONV_DIM), lambda b, c: (b, c, cblk)),
            pl.BlockSpec((None, SUBLANES, SSD_CONV_DIM),
                         lambda b, c: (b, jnp.minimum((c + 1) * rb, S // SUBLANES - 1), cblk)),
            pl.BlockSpec((SUBLANES, SSD_CONV_DIM), lambda b, c: (0, 0)),
            pl.BlockSpec((1, SSD_CONV_DIM), lambda b, c: (0, 0)),
        ],
        out_specs=pl.BlockSpec((None, L, SSD_CONV_DIM), lambda b, c: (b, c, 0)),
        out_shape=jax.ShapeDtypeStruct((B, S, SSD_CONV_DIM), BF16),
        scratch_shapes=[pltpu.VMEM((L + 2 * SUBLANES, SSD_CONV_DIM), F32)],
        compiler_params=_cparams(("parallel", "parallel")),
        name="ssd_conv",
    )(proj3, proj3, proj3, cw, cb)


def _ssd_sweep_kernel(xbc_ref, dt_ref, dtb_ref, alog_ref, ex_ref, y_ref, state_ref, *, rev, hoff):
    c = pl.program_id(1)
    L = SSD_CHUNK

    @pl.when(c == 0)
    def _():
        state_ref[...] = jnp.zeros_like(state_ref)

    xbc = xbc_ref[...]
    xs = xbc[:, :SSD_D_INNER].astype(F32)
    Bm = xbc[:, SSD_D_INNER:SSD_D_INNER + LANES]
    Cm = xbc[:, SSD_D_INNER + LANES:SSD_D_INNER + 2 * LANES]
    ex = ex_ref[...]

    dt = jax.nn.softplus(dt_ref[...] + dtb_ref[...])
    a = dt * (-jnp.exp(alog_ref[...]))
    row = lax.broadcasted_iota(jnp.int32, (L, L), 0)
    col = lax.broadcasted_iota(jnp.int32, (L, L), 1)
    tri = (col >= row) if rev else (col <= row)
    cs = _exact_dot_lhs(tri.astype(BF16), a)
    csT = cs.T
    end = 0 if rev else L - 1
    cs_end = cs[end:end + 1, :]

    xdt = xs * _exact_dot_rhs(dt, ex)
    xdt_b = xdt.astype(BF16)
    xw = (xdt * _exact_dot_rhs(jnp.exp(cs_end - cs), ex)).astype(BF16)

    lane = lax.broadcasted_iota(jnp.int32, (1, LANES), 1)
    g0 = lane < SSD_D_STATE
    zero_b = jnp.zeros((), BF16)
    cb0 = _dot_nt(jnp.where(g0, Cm, zero_b), Bm)
    cb1 = _dot_nt(jnp.where(g0, zero_b, Cm), Bm)

    ys = []
    for pair in range(SSD_N_HEADS // 2):
        ms = []
        for hh in (2 * pair, 2 * pair + 1):
            k = hoff + hh
            seg = cs[:, k:k + 1] - csT[k:k + 1, :]
            dec = jnp.exp(jnp.where(tri, seg, -jnp.inf))
            cbg = cb0 if hh < SSD_N_HEADS // 2 else cb1
            ms.append((cbg * dec).astype(BF16))
        lhs = jnp.concatenate(ms, axis=1)
        xp = xdt_b[:, LANES * pair:LANES * (pair + 1)]
        rhs = jnp.concatenate([jnp.where(g0, xp, zero_b), jnp.where(g0, zero_b, xp)], axis=0)
        ys.append(_dot(lhs, rhs))
    y_diag = jnp.concatenate(ys, axis=1)

    st = state_ref[...]
    y_off = _dot(Cm, st.astype(BF16)) * _exact_dot_rhs(jnp.exp(cs), ex)
    y_ref[...] = y_diag + y_off

    dec_c = _exact_dot_rhs(jnp.broadcast_to(jnp.exp(cs_end), (SUBLANES, LANES)), ex)[0:1, :]
    BmT = Bm.astype(F32).T.astype(BF16)
    upd = _dot(BmT, xw)
    r2 = lax.broadcasted_iota(jnp.int32, (LANES, SSD_D_INNER), 0) // SSD_D_STATE
    c2 = lax.broadcasted_iota(jnp.int32, (LANES, SSD_D_INNER), 1) // (SSD_D_INNER // SSD_N_GROUPS)
    state_ref[...] = st * dec_c + jnp.where(r2 == c2, upd, 0.0)


def _ssd_sweep(xbc3, dt3, dtb_row, alog_row, ex, rev):
    B, S, _ = xbc3.shape
    L = SSD_CHUNK
    nc = S // L
    hoff = SSD_N_HEADS if rev else 0
    if rev:
        cmap = lambda b, c: (b, nc - 1 - c, 0)
    else:
        cmap = lambda b, c: (b, c, 0)
    return pl.pallas_call(
        functools.partial(_ssd_sweep_kernel, rev=rev, hoff=hoff),
        grid=(B, nc),
        in_specs=[
            pl.BlockSpec((None, L, SSD_CONV_DIM), cmap),
            pl.BlockSpec((None, L, LANES), cmap),
            pl.BlockSpec((1, LANES), lambda b, c: (0, 0)),
            pl.BlockSpec((1, LANES), lambda b, c: (0, 0)),
            pl.BlockSpec((LANES, SSD_D_INNER), lambda b, c: (0, 0)),
        ],
        out_specs=pl.BlockSpec((None, L, SSD_D_INNER), cmap),
        out_shape=jax.ShapeDtypeStruct((B, S, SSD_D_INNER), F32),
        scratch_shapes=[pltpu.VMEM((LANES, SSD_D_INNER), F32)],
        compiler_params=_cparams(("parallel", "arbitrary")),
        name="ssd_sweep_rev" if rev else "ssd_sweep_fwd",
    )(xbc3, dt3, dtb_row, alog_row, ex)


def _attn_kernel(q_ref, k_ref, v_ref, bias_ref, qg_ref, kg_ref, sg_ref, lam_ref, o_ref, kn_ref,
                 *, nk, scale, out_scale):
    qi = pl.program_id(2)
    tq = q_ref.shape[0]
    r = lax.broadcasted_iota(jnp.int32, (LANES, LANES), 0) // ATTN_HEAD_DIM
    c = lax.broadcasted_iota(jnp.int32, (LANES, LANES), 1) // ATTN_HEAD_DIM
    ones2 = (r == c).astype(BF16)

    def halfnorm(x, g):
        ms = _exact_dot_rhs(x * x, ones2) * (1.0 / ATTN_HEAD_DIM)
        return x * lax.rsqrt(ms + EPS) * g

    @pl.when(qi == 0)
    def _():
        kn_ref[...] = halfnorm(k_ref[...].astype(F32), kg_ref[...]).astype(BF16)

    qn = (halfnorm(q_ref[...].astype(F32), qg_ref[...]) * scale).astype(BF16)
    lane = lax.broadcasted_iota(jnp.int32, (1, LANES), 1)
    first = lane < ATTN_HEAD_DIM
    zero_b = jnp.zeros((), BF16)
    q1 = jnp.where(first, qn, zero_b)
    q2 = jnp.where(first, zero_b, qn)

    def step(s, v, m, l, acc):
        mn = jnp.maximum(m, jnp.max(s, axis=1, keepdims=True))
        alpha = jnp.exp(m - mn)
        p = jnp.exp(s - mn)
        l = alpha * l + jnp.sum(p, axis=1, keepdims=True)
        acc = alpha * acc + _dot(p.astype(BF16), v)
        return mn, l, acc

    def body(kb, carry):
        m1, l1, a1, m2, l2, a2 = carry
        off = pl.multiple_of(kb * LANES, LANES)
        kblk = kn_ref[pl.ds(off, LANES), :]
        vblk = v_ref[pl.ds(off, LANES), :]
        bias = bias_ref[jnp.clip(kb - qi, -2, 2) + 2]
        m1, l1, a1 = step(_dot_nt(q1, kblk) + bias, vblk, m1, l1, a1)
        m2, l2, a2 = step(_dot_nt(q2, kblk) + bias, vblk, m2, l2, a2)
        return m1, l1, a1, m2, l2, a2

    minf = jnp.full((tq, 1), -jnp.inf, F32)
    zl = jnp.zeros((tq, 1), F32)
    za = jnp.zeros((tq, LANES), F32)
    m1, l1, a1, m2, l2, a2 = lax.fori_loop(0, nk, body, (minf, zl, za, minf, zl, za))
    o = a1 / l1 - lam_ref[...] * (a2 / l2)
    ms = jnp.mean(o * o, axis=-1, keepdims=True)
    o_ref[...] = (o * lax.rsqrt(ms + EPS) * sg_ref[...] * out_scale).astype(BF16)


def _attention(proj3, bias5, qg, kg, sg, lam_row, lam_init):
    B, S, _ = proj3.shape
    tq = LANES
    nq = S // tq
    return pl.pallas_call(
        functools.partial(_attn_kernel, nk=S // LANES, scale=ATTN_HEAD_DIM ** -0.5, out_scale=1.0 - lam_init),
        grid=(B, ATTN_HEADS, nq),
        in_specs=[
            pl.BlockSpec((None, tq, LANES), lambda b, h, q: (b, q, COL_Q // LANES + h)),
            pl.BlockSpec((None, S, LANES), lambda b, h, q: (b, 0, COL_K // LANES + h)),
            pl.BlockSpec((None, S, LANES), lambda b, h, q: (b, 0, COL_V // LANES + h)),
            pl.BlockSpec((None, 5, LANES, LANES), lambda b, h, q: (h, 0, 0, 0)),
            pl.BlockSpec((1, LANES), lambda b, h, q: (0, 0)),
            pl.BlockSpec((1, LANES), lambda b, h, q: (0, 0)),
            pl.BlockSpec((1, LANES), lambda b, h, q: (0, 0)),
            pl.BlockSpec((1, LANES), lambda b, h, q: (0, 0)),
        ],
        out_specs=pl.BlockSpec((None, tq, LANES), lambda b, h, q: (b, q, h)),
        out_shape=jax.ShapeDtypeStruct((B, S, ATTN_HEADS * ATTN_V_DIM), BF16),
        scratch_shapes=[pltpu.VMEM((S, LANES), BF16)],
        compiler_params=_cparams(("parallel", "parallel", "arbitrary")),
        name="diff_attn",
    )(proj3, proj3, proj3, bias5, qg, kg, sg, lam_row)


def _mix_kernel(x_ref, yf_ref, yb_ref, xbc_ref, z_ref, at_ref, gs_ref, ga_ref, dsk_ref, ng_ref,
                wssd_ref, wattn_ref, wout_ref, h_ref):
    xs = xbc_ref[...][:, :SSD_D_INNER].astype(F32)
    y = yf_ref[...] + yb_ref[...] + xs * dsk_ref[...]
    z = z_ref[...].astype(F32)
    y = y * (z * jax.nn.sigmoid(z))
    gw = SSD_D_INNER // SSD_N_GROUPS
    parts = []
    for g in range(SSD_N_GROUPS):
        yg = y[:, g * gw:(g + 1) * gw]
        parts.append(yg * lax.rsqrt(jnp.mean(yg * yg, axis=-1, keepdims=True) + EPS))
    yn = (jnp.concatenate(parts, axis=1) * ng_ref[...]).astype(BF16)
    y_ssd = _dot(yn, wssd_ref[...])
    y_attn = _dot(at_ref[...], wattn_ref[...])
    mixed = (jax.nn.sigmoid(gs_ref[...].astype(F32)) * y_ssd
             + jax.nn.sigmoid(ga_ref[...].astype(F32)) * y_attn)
    h_ref[...] = x_ref[...] + _dot(mixed.astype(BF16), wout_ref[...])


def _mix(x2, yf2, yb2, xbc2, proj2, attn2, dsk_row, ng_row, wssd, wattn, wout, tm):
    T = x2.shape[0]
    row = lambda i: (i, 0)
    const = lambda i: (0, 0)
    return pl.pallas_call(
        _mix_kernel,
        grid=(T // tm,),
        in_specs=[
            pl.BlockSpec((tm, D_MODEL), row),
            pl.BlockSpec((tm, SSD_D_INNER), row),
            pl.BlockSpec((tm, SSD_D_INNER), row),
            pl.BlockSpec((tm, SSD_CONV_DIM), row),
            pl.BlockSpec((tm, D_MODEL), lambda i: (i, COL_Z // D_MODEL)),
            pl.BlockSpec((tm, D_MODEL), row),
            pl.BlockSpec((tm, D_MODEL), lambda i: (i, COL_GS // D_MODEL)),
            pl.BlockSpec((tm, D_MODEL), lambda i: (i, COL_GA // D_MODEL)),
            pl.BlockSpec((1, D_MODEL), const),
            pl.BlockSpec((1, D_MODEL), const),
            pl.BlockSpec((D_MODEL, D_MODEL), const),
            pl.BlockSpec((D_MODEL, D_MODEL), const),
            pl.BlockSpec((D_MODEL, D_MODEL), const),
        ],
        out_specs=pl.BlockSpec((tm, D_MODEL), row),
        out_shape=jax.ShapeDtypeStruct((T, D_MODEL), F32),
        compiler_params=_cparams(("parallel",)),
        name="mix",
    )(x2, yf2, yb2, xbc2, proj2, attn2, proj2, proj2, dsk_row, ng_row, wssd, wattn, wout)


def _extract_topk(work, n):
    iota = lax.broadcasted_iota(jnp.int32, work.shape, 0)
    big = jnp.int32(work.shape[0])
    rank = jnp.full(work.shape, n, jnp.int32)
    vals = []
    for r in range(n):
        m = jnp.max(work, axis=0, keepdims=True)
        idx = jnp.min(jnp.where(work == m, iota, big), axis=0, keepdims=True)
        sel = iota == idx
        rank = jnp.where(sel, r, rank)
        work = jnp.where(sel, -jnp.inf, work)
        vals.append(m)
    return rank, vals


def _peer_kernel(h_ref, g2_ref, wqT_ref, sk_ref, u_ref, vT_ref, o_ref,
                 hnT_ref, qT_ref, rb_ref, eb_ref, ni_ref, fi_ref, acc_ref, cand_ref):
    i = pl.program_id(1)
    K = PEER_TOPK
    tb = h_ref.shape[0]

    @pl.when(i == 0)
    def _router():
        h = h_ref[...]
        hn = h * lax.rsqrt(jnp.mean(h * h, axis=-1, keepdims=True) + EPS) * g2_ref[...]
        hnT = hn.T.astype(BF16)
        hnT_ref[...] = hnT
        qT_ref[...] = _dot(wqT_ref[...], hnT).astype(BF16)
        acc_ref[...] = jnp.zeros_like(acc_ref)
        cand_ref[...] = jnp.full(cand_ref.shape, -jnp.inf, F32)

        def head_body(hd, carry):
            ranks, tops, scores = [], [], []
            for m in range(2):
                off = pl.multiple_of(hd * PEER_N_KEYS + m * (PEER_N_KEYS // 2), PEER_N_KEYS // 2)
                s = _dot(sk_ref[m], qT_ref[pl.ds(off, PEER_N_KEYS // 2), :])
                rank, vals = _extract_topk(s, K)
                ranks.append(rank)
                tops.append(vals)
                scores.append(s)
            for r, (a, b) in enumerate(_CAND):
                cand_ref[r:r + 1, :] = tops[0][a] + tops[1][b]
            cand = cand_ref[...]
            crank, cvals = _extract_topk(cand, K)
            csel = crank < K
            mx = cvals[0]
            zsum = jnp.sum(jnp.where(csel, jnp.exp(cand - mx), 0.0), axis=0, keepdims=True)
            inv_z = 1.0 / zsum
            crow = lax.broadcasted_iota(jnp.int32, cand.shape, 0)
            self_f = csel.astype(F32)
            ni = jnp.zeros((PEER_N_KEYS, tb), F32)
            r0 = 0
            for a in range(K):
                r1 = r0 + K // (a + 1)
                n_a = jnp.sum(jnp.where((crow >= r0) & (crow < r1), self_f, 0.0), axis=0, keepdims=True)
                ni = ni + jnp.where(ranks[0] == a, n_a, 0.0)
                r0 = r1
            ni_ref[hd] = ni
            fi_ref[hd] = jnp.where(ranks[0] < K, jnp.exp(scores[0] - tops[0][0]), 0.0)
            rb_ref[hd] = ranks[1].astype(F32).astype(BF16)
            eb_ref[hd] = jnp.where(ranks[1] < K, jnp.exp(scores[1] - tops[1][0]) * inv_z, 0.0).astype(BF16)
            return carry

        lax.fori_loop(0, PEER_HEADS, head_body, 0)

    hid = _dot(u_ref[...], hnT_ref[...])
    ge = 0.5 * hid * (1.0 + lax.erf(hid * (2.0 ** -0.5)))
    gate = jnp.zeros(hid.shape, BF16)
    for hd in range(PEER_HEADS):
        n_row = ni_ref[hd, pl.ds(i, 1), :].astype(BF16)
        f_row = fi_ref[hd, pl.ds(i, 1), :].astype(BF16)
        gate = gate + jnp.where(rb_ref[hd] < n_row, eb_ref[hd] * f_row, jnp.zeros((), BF16))
    w = (ge * gate.astype(F32)).astype(BF16)
    acc_ref[...] += _dot(vT_ref[...], w)

    @pl.when(i == pl.num_programs(1) - 1)
    def _():
        o_ref[...] = h_ref[...] + acc_ref[...].T


def _peer(h2, g2_row, wqT, sk, u_b, vT_b, tb):
    T = h2.shape[0]
    nk = PEER_N_KEYS
    return pl.pallas_call(
        _peer_kernel,
        grid=(T // tb, nk),
        in_specs=[
            pl.BlockSpec((tb, D_MODEL), lambda t, i: (t, 0)),
            pl.BlockSpec((1, D_MODEL), lambda t, i: (0, 0)),
            pl.BlockSpec((D_MODEL, D_MODEL), lambda t, i: (0, 0)),
            pl.BlockSpec((2, nk, nk // 2), lambda t, i: (0, 0, 0)),
            pl.BlockSpec((nk, D_MODEL), lambda t, i: (i, 0)),
            pl.BlockSpec((D_MODEL, nk), lambda t, i: (0, i)),
        ],
        out_specs=pl.BlockSpec((tb, D_MODEL), lambda t, i: (t, 0)),
        out_shape=jax.ShapeDtypeStruct((T, D_MODEL), F32),
        scratch_shapes=[
            pltpu.VMEM((D_MODEL, tb), BF16),
            pltpu.VMEM((D_MODEL, tb), BF16),
            pltpu.VMEM((PEER_HEADS, nk, tb), BF16),
            pltpu.VMEM((PEER_HEADS, nk, tb), BF16),
            pltpu.VMEM((PEER_HEADS, nk, tb), F32),
            pltpu.VMEM((PEER_HEADS, nk, tb), F32),
            pltpu.VMEM((D_MODEL, tb), F32),
            pltpu.VMEM((_CAND_ROWS, tb), F32),
        ],
        compiler_params=_cparams(("parallel", "arbitrary")),
        name="peer",
    )(h2, g2_row, wqT, sk, u_b, vT_b)


def _rel_bucket(rel):
    nb = REL_BUCKETS // 2
    ret = jnp.where(rel > 0, nb, 0)
    n = jnp.abs(rel)
    max_exact = nb // 2
    nf = jnp.maximum(n, 1).astype(F32)
    large = max_exact + (jnp.log(nf / max_exact) / math.log(REL_MAX_DIST / max_exact)
                         * (nb - max_exact)).astype(jnp.int32)
    large = jnp.minimum(large, nb - 1)
    return ret + jnp.where(n < max_exact, n, large)


def _bias_tiles(rel_bias):
    d = jnp.arange(-2, 3, dtype=jnp.int32)[:, None, None] * LANES
    q = jnp.arange(LANES, dtype=jnp.int32)[None, :, None]
    k = jnp.arange(LANES, dtype=jnp.int32)[None, None, :]
    bucket = _rel_bucket(d + k - q)
    return rel_bias[bucket].astype(F32).transpose(3, 0, 1, 2)


def _pad_rows(a, n):
    return jnp.pad(a, ((0, n - a.shape[0]), (0, 0)))


def _prep(rel_bias, norm1_gain, w_in, conv_w, conv_b, dt_bias_f, dt_bias_b, a_log_f, a_log_b, d_skip,
          ssd_norm_gain, w_ssd_proj, q_norm_gain, k_norm_gain, lambda_q1, lambda_k1, lambda_q2, lambda_k2,
          subln_gain, w_attn_proj, w_out, norm2_gain, peer_w_q, peer_sub_keys, peer_u, peer_v, lam_init):
    o1 = SSD_D_INNER
    o2 = o1 + SSD_CONV_DIM
    o3 = o2 + SSD_N_HEADS
    o4 = o3 + SSD_N_HEADS
    o5 = o4 + 1024
    o6 = o5 + 1024
    o7 = o6 + 1024
    o8 = o7 + D_MODEL
    w = w_in
    zeros = jnp.zeros((D_MODEL, COL_XBC - (COL_GA + D_MODEL)), F32)
    slab = jnp.concatenate([w[:, :o1], w[:, o4:o5], w[:, o5:o6], w[:, o6:o7], w[:, o7:o8], w[:, o8:],
                            zeros, w[:, o1:o2]], axis=1).astype(BF16)
    w_dt = jnp.pad(w[:, o2:o4], ((0, 0), (0, LANES - 2 * SSD_N_HEADS))).astype(BF16)
    row128 = lambda f, b: jnp.pad(jnp.concatenate([f, b]), (0, LANES - 2 * SSD_N_HEADS)).reshape(1, LANES)
    hp = jnp.arange(SSD_D_INNER, dtype=jnp.int32) // SSD_HEAD_DIM
    rows = jnp.arange(LANES, dtype=jnp.int32)[:, None]
    f32 = F32
    lam = (jnp.exp(jnp.sum(lambda_q1.astype(f32) * lambda_k1.astype(f32)))
           - jnp.exp(jnp.sum(lambda_q2.astype(f32) * lambda_k2.astype(f32))) + lam_init)
    return dict(
        gain1=norm1_gain.reshape(1, D_MODEL),
        slab=slab,
        w_dt=w_dt,
        cw=_pad_rows(conv_w.reshape(SSD_CONV, SSD_CONV_DIM), SUBLANES),
        cb=conv_b.reshape(1, SSD_CONV_DIM),
        dtb=row128(dt_bias_f, dt_bias_b),
        alog=row128(a_log_f, a_log_b),
        ex_f=(rows == hp[None, :]).astype(BF16),
        ex_b=(rows == hp[None, :] + SSD_N_HEADS).astype(BF16),
        dsk=jnp.repeat(d_skip, SSD_HEAD_DIM).reshape(1, SSD_D_INNER),
        ng=ssd_norm_gain.reshape(1, SSD_D_INNER),
        wssd=w_ssd_proj.astype(BF16),
        wattn=w_attn_proj.astype(BF16),
        wout=w_out.astype(BF16),
        bias5=_bias_tiles(rel_bias),
        qg=jnp.tile(q_norm_gain, 2).reshape(1, LANES),
        kg=jnp.tile(k_norm_gain, 2).reshape(1, LANES),
        sg=subln_gain.reshape(1, LANES),
        lam=jnp.broadcast_to(lam, (1, LANES)).astype(F32),
        g2=norm2_gain.reshape(1, D_MODEL),
        wqT=peer_w_q.T.astype(BF16),
        sk=peer_sub_keys.astype(BF16),
        u=peer_u.astype(BF16),
        vT=peer_v.T.astype(BF16),
    )


def _largest_divisor(n, cap):
    t = cap
    while n % t:
        t //= 2
    return t


def _layer(x, p, lam_init):
    B, S, _ = x.shape
    T = B * S
    x2 = x.reshape(T, D_MODEL)
    tm = _largest_divisor(T, 1024)
    proj2, dt2 = _inproj(x2, p["gain1"], p["slab"], p["w_dt"], tm)
    proj3 = proj2.reshape(B, S, PROJ_W)
    dt3 = dt2.reshape(B, S, LANES)
    xbc3 = _conv(proj3, p["cw"], p["cb"])
    yf = _ssd_sweep(xbc3, dt3, p["dtb"], p["alog"], p["ex_f"], rev=False)
    yb = _ssd_sweep(xbc3, dt3, p["dtb"], p["alog"], p["ex_b"], rev=True)
    attn = _attention(proj3, p["bias5"], p["qg"], p["kg"], p["sg"], p["lam"], lam_init)
    h2 = _mix(x2, yf.reshape(T, -1), yb.reshape(T, -1), xbc3.reshape(T, -1), proj2, attn.reshape(T, -1),
              p["dsk"], p["ng"], p["wssd"], p["wattn"], p["wout"], _largest_divisor(T, 512))
    y2 = _peer(h2, p["g2"], p["wqT"], p["sk"], p["u"], p["vT"], _largest_divisor(T, 512))
    return y2.reshape(B, S, D_MODEL)


def kernel(x_prompt, x_sample, rel_bias, norm1_gain, w_in, conv_w, conv_b, dt_bias_f, dt_bias_b, a_log_f, a_log_b, d_skip, ssd_norm_gain, w_ssd_proj, q_norm_gain, k_norm_gain, lambda_q1, lambda_k1, lambda_q2, lambda_k2, subln_gain, w_attn_proj, w_out, norm2_gain, peer_w_q, peer_sub_keys, peer_u, peer_v):
    depth = norm1_gain.shape[0]
    lam_inits = [0.8 - 0.6 * math.exp(-0.3 * i) for i in range(depth)]
    preps = [
        _prep(rel_bias, norm1_gain[i], w_in[i], conv_w[i], conv_b[i], dt_bias_f[i], dt_bias_b[i],
              a_log_f[i], a_log_b[i], d_skip[i], ssd_norm_gain[i], w_ssd_proj[i], q_norm_gain[i],
              k_norm_gain[i], lambda_q1[i], lambda_k1[i], lambda_q2[i], lambda_k2[i], subln_gain[i],
              w_attn_proj[i], w_out[i], norm2_gain[i], peer_w_q[i], peer_sub_keys[i], peer_u[i],
              peer_v[i], lam_inits[i])
        for i in range(depth)
    ]
    outs = []
    for x in (x_prompt, x_sample):
        for p, lam_init in zip(preps, lam_inits):
            x = _layer(x, p, lam_init)
        outs.append(x)
    return tuple(outs)
```

```python
import functools
import math

import jax
import jax.numpy as jnp
from jax import lax
from jax.experimental import pallas as pl
from jax.experimental.pallas import tpu as pltpu

F32 = jnp.float32
BF16 = jnp.bfloat16

D_MODEL = 1024
SSD_D_INNER = 1024
SSD_HEAD_DIM = 64
SSD_N_HEADS = 16
SSD_N_GROUPS = 2
SSD_D_STATE = 64
SSD_CONV = 5
SSD_CHUNK = 128
SSD_CONV_DIM = 1280
ATTN_HEADS = 8
ATTN_HEAD_DIM = 64
ATTN_V_DIM = 128
REL_BUCKETS = 32
REL_MAX_DIST = 128
PEER_N_KEYS = 128
PEER_HEADS = 8
PEER_TOPK = 16
EPS = 1e-6

LANES = 128
SUBLANES = 8
VMEM_LIMIT = 56 * 1024 * 1024

COL_Z = 0
COL_Q = 1024
COL_K = 2048
COL_V = 3072
COL_GS = 4096
COL_GA = 5120
COL_XBC = 6400
PROJ_W = 7680
PROJ_TN = 1536

_CAND = [(a, b) for a in range(PEER_TOPK) for b in range(PEER_TOPK) if (a + 1) * (b + 1) <= PEER_TOPK]
_CAND_ROWS = 56


def _dot(a, b):
    return jnp.dot(a, b, preferred_element_type=F32)


def _dot_nt(a, b):
    return lax.dot_general(a, b, (((1,), (1,)), ((), ())), preferred_element_type=F32)


def _split3(v):
    hi = v.astype(BF16)
    r = v - hi.astype(F32)
    mid = r.astype(BF16)
    lo = (r - mid.astype(F32)).astype(BF16)
    return hi, mid, lo


def _exact_dot_rhs(v, m):
    hi, mid, lo = _split3(v)
    return _dot(hi, m) + _dot(mid, m) + _dot(lo, m)


def _exact_dot_lhs(m, v):
    hi, mid, lo = _split3(v)
    return _dot(m, hi) + _dot(m, mid) + _dot(m, lo)


def _cparams(sem, flags=None):
    return pltpu.CompilerParams(dimension_semantics=sem, vmem_limit_bytes=VMEM_LIMIT, flags=flags)


def _inproj_kernel(x_ref, g_ref, w_ref, wdt_ref, o_ref, dt_ref, xn_ref):
    j = pl.program_id(1)

    @pl.when(j == 0)
    def _():
        x = x_ref[...]
        ms = jnp.mean(x * x, axis=-1, keepdims=True)
        xn = (x * lax.rsqrt(ms + EPS) * g_ref[...]).astype(BF16)
        xn_ref[...] = xn
        dt_ref[...] = _dot(xn, wdt_ref[...])

    o_ref[...] = _dot(xn_ref[...], w_ref[...]).astype(BF16)


def _inproj(x2, gain, w_slab, w_dt, tm):
    T = x2.shape[0]
    grid = (T // tm, PROJ_W // PROJ_TN)
    return pl.pallas_call(
        _inproj_kernel,
        grid=grid,
        in_specs=[
            pl.BlockSpec((tm, D_MODEL), lambda i, j: (i, 0)),
            pl.BlockSpec((1, D_MODEL), lambda i, j: (0, 0)),
            pl.BlockSpec((D_MODEL, PROJ_TN), lambda i, j: (0, j)),
            pl.BlockSpec((D_MODEL, LANES), lambda i, j: (0, 0)),
        ],
        out_specs=[
            pl.BlockSpec((tm, PROJ_TN), lambda i, j: (i, j)),
            pl.BlockSpec((tm, LANES), lambda i, j: (i, 0)),
        ],
        out_shape=[
            jax.ShapeDtypeStruct((T, PROJ_W), BF16),
            jax.ShapeDtypeStruct((T, LANES), F32),
        ],
        scratch_shapes=[pltpu.VMEM((tm, D_MODEL), BF16)],
        compiler_params=_cparams(("parallel", "arbitrary")),
        name="inproj",
    )(x2, gain, w_slab, w_dt)


def _conv_kernel(xp_ref, xc_ref, xn_ref, cw_ref, cb_ref, o_ref, pad_ref, *, nc):
    c = pl.program_id(1)
    L = SSD_CHUNK
    H = SUBLANES
    xp = jnp.where(c > 0, xp_ref[...].astype(F32), 0.0)
    xn = jnp.where(c < nc - 1, xn_ref[...].astype(F32), 0.0)
    pad_ref[0:H, :] = xp
    pad_ref[H:H + L, :] = xc_ref[...].astype(F32)
    pad_ref[H + L:2 * H + L, :] = xn
    half = (SSD_CONV - 1) // 2
    acc = jnp.broadcast_to(cb_ref[...], (L, SSD_CONV_DIM))
    for k in range(SSD_CONV):
        off = H - half + k
        acc = acc + cw_ref[k:k + 1, :] * pad_ref[off:off + L, :]
    o_ref[...] = (acc * jax.nn.sigmoid(acc)).astype(BF16)


def _conv(proj3, cw, cb):
    B, S, _ = proj3.shape
    L = SSD_CHUNK
    nc = S // L
    rb = L // SUBLANES
    cblk = COL_XBC // SSD_CONV_DIM
    return pl.pallas_call(
        functools.partial(_conv_kernel, nc=nc),
        grid=(B, nc),
        in_specs=[
            pl.BlockSpec((None, SUBLANES, SSD_CONV_DIM), lambda b, c: (b, jnp.maximum(c * rb - 1, 0), cblk)),
            pl.BlockSpec((None, L, SSD_CONV_DIM), lambda b, c: (b, c, cblk)),
            pl.BlockSpec((None, SUBLANES, SSD_CONV_DIM),
                         lambda b, c: (b, jnp.minimum((c + 1) * rb, S // SUBLANES - 1), cblk)),
            pl.BlockSpec((SUBLANES, SSD_CONV_DIM), lambda b, c: (0, 0)),
            pl.BlockSpec((1, SSD_CONV_DIM), lambda b, c: (0, 0)),
        ],
        out_specs=pl.BlockSpec((None, L, SSD_CONV_DIM), lambda b, c: (b, c, 0)),
        out_shape=jax.ShapeDtypeStruct((B, S, SSD_CONV_DIM), BF16),
        scratch_shapes=[pltpu.VMEM((L + 2 * SUBLANES, SSD_CONV_DIM), F32)],
        compiler_params=_cparams(("parallel", "parallel")),
        name="ssd_conv",
    )(proj3, proj3, proj3, cw, cb)


def _ssd_sweep_kernel(xbc_ref, dt_ref, dtb_ref, alog_ref, ex_ref, y_ref, state_ref, *, rev, hoff):
    c = pl.program_id(1)
    L = SSD_CHUNK

    @pl.when(c == 0)
    def _():
        state_ref[...] = jnp.zeros_like(state_ref)

    xbc = xbc_ref[...]
    xs = xbc[:, :SSD_D_INNER].astype(F32)
    Bm = xbc[:, SSD_D_INNER:SSD_D_INNER + LANES]
    Cm = xbc[:, SSD_D_INNER + LANES:SSD_D_INNER + 2 * LANES]
    ex = ex_ref[...]

    dt = jax.nn.softplus(dt_ref[...] + dtb_ref[...])
    a = dt * (-jnp.exp(alog_ref[...]))
    row = lax.broadcasted_iota(jnp.int32, (L, L), 0)
    col = lax.broadcasted_iota(jnp.int32, (L, L), 1)
    tri = (col >= row) if rev else (col <= row)
    cs = _exact_dot_lhs(tri.astype(BF16), a)
    csT = cs.T
    end = 0 if rev else L - 1
    cs_end = cs[end:end + 1, :]

    xdt = xs * _exact_dot_rhs(dt, ex)
    xdt_b = xdt.astype(BF16)
    xw = (xdt * _exact_dot_rhs(jnp.exp(cs_end - cs), ex)).astype(BF16)

    lane = lax.broadcasted_iota(jnp.int32, (1, LANES), 1)
    g0 = lane < SSD_D_STATE
    zero_b = jnp.zeros((), BF16)
    cb0 = _dot_nt(jnp.where(g0, Cm, zero_b), Bm)
    cb1 = _dot_nt(jnp.where(g0, zero_b, Cm), Bm)

    ys = []
    for pair in range(SSD_N_HEADS // 2):
        ms = []
        for hh in (2 * pair, 2 * pair + 1):
            k = hoff + hh
            seg = cs[:, k:k + 1] - csT[k:k + 1, :]
            dec = jnp.exp(jnp.where(tri, seg, -jnp.inf))
            cbg = cb0 if hh < SSD_N_HEADS // 2 else cb1
            ms.append((cbg * dec).astype(BF16))
        lhs = jnp.concatenate(ms, axis=1)
        xp = xdt_b[:, LANES * pair:LANES * (pair + 1)]
        rhs = jnp.concatenate([jnp.where(g0, xp, zero_b), jnp.where(g0, zero_b, xp)], axis=0)
        ys.append(_dot(lhs, rhs))
    y_diag = jnp.concatenate(ys, axis=1)

    st = state_ref[...]
    y_off = _dot(Cm, st.astype(BF16)) * _exact_dot_rhs(jnp.exp(cs), ex)
    y_ref[...] = y_diag + y_off

    dec_c = _exact_dot_rhs(jnp.broadcast_to(jnp.exp(cs_end), (SUBLANES, LANES)), ex)[0:1, :]
    BmT = Bm.astype(F32).T.astype(BF16)
    upd = _dot(BmT, xw)
    r2 = lax.broadcasted_iota(jnp.int32, (LANES, SSD_D_INNER), 0) // SSD_D_STATE
    c2 = lax.broadcasted_iota(jnp.int32, (LANES, SSD_D_INNER), 1) // (SSD_D_INNER // SSD_N_GROUPS)
    state_ref[...] = st * dec_c + jnp.where(r2 == c2, upd, 0.0)


def _ssd_sweep(xbc3, dt3, dtb_row, alog_row, ex, rev):
    B, S, _ = xbc3.shape
    L = SSD_CHUNK
    nc = S // L
    hoff = SSD_N_HEADS if rev else 0
    if rev:
        cmap = lambda b, c: (b, nc - 1 - c, 0)
    else:
        cmap = lambda b, c: (b, c, 0)
    return pl.pallas_call(
        functools.partial(_ssd_sweep_kernel, rev=rev, hoff=hoff),
        grid=(B, nc),
        in_specs=[
            pl.BlockSpec((None, L, SSD_CONV_DIM), cmap),
            pl.BlockSpec((None, L, LANES), cmap),
            pl.BlockSpec((1, LANES), lambda b, c: (0, 0)),
            pl.BlockSpec((1, LANES), lambda b, c: (0, 0)),
            pl.BlockSpec((LANES, SSD_D_INNER), lambda b, c: (0, 0)),
        ],
        out_specs=pl.BlockSpec((None, L, SSD_D_INNER), cmap),
        out_shape=jax.ShapeDtypeStruct((B, S, SSD_D_INNER), F32),
        scratch_shapes=[pltpu.VMEM((LANES, SSD_D_INNER), F32)],
        compiler_params=_cparams(("parallel", "arbitrary")),
        name="ssd_sweep_rev" if rev else "ssd_sweep_fwd",
    )(xbc3, dt3, dtb_row, alog_row, ex)


ATTN_TQ = 256
ATTN_KB = LANES
ATTN_PV_TILES = 4
ATTN_QK_UNROLL = 4
LOG2E = math.log2(math.e)


def _attn_kernel(q_ref, k_ref, v_ref, bias_ref, cfar_ref, qg_ref, kg_ref, sg_ref, lam_ref, o_ref,
                 kn_ref, s1_ref, s2_ref, mx1_ref, mx2_ref, *, nk, scale, out_scale):
    qi = pl.program_id(2)
    tq = q_ref.shape[0]
    kb = ATTN_KB
    n_near = tq // kb + 2
    r = lax.broadcasted_iota(jnp.int32, (LANES, LANES), 0) // ATTN_HEAD_DIM
    c = lax.broadcasted_iota(jnp.int32, (LANES, LANES), 1) // ATTN_HEAD_DIM
    ones2 = (r == c).astype(BF16)
    lane = lax.broadcasted_iota(jnp.int32, (1, LANES), 1)
    first = lane < ATTN_HEAD_DIM
    zero_b = jnp.zeros((), BF16)

    def halfnorm(x, g):
        ms = _exact_dot_rhs(x * x, ones2) * (1.0 / ATTN_HEAD_DIM)
        return x * lax.rsqrt(ms + EPS) * g

    @pl.when(qi == 0)
    def _():
        cl = cfar_ref[0:1, :]
        cr = cfar_ref[1:2, :]
        cl_hi = cl.astype(BF16).astype(F32)
        cr_hi = cr.astype(BF16).astype(F32)
        crow = jnp.where(lane == 0, cl_hi, jnp.where(lane == 1, cl - cl_hi,
                         jnp.where(lane == 2, cr_hi, jnp.where(lane == 3, cr - cr_hi, 0.0))))
        cpart = jnp.broadcast_to(crow, (kb, LANES)).astype(BF16)

        def prep(t, carry):
            src = pl.multiple_of(t * kb, kb)
            dst = pl.multiple_of(t * 2 * kb, 2 * kb)
            kn = halfnorm(k_ref[pl.ds(src, kb), :].astype(F32), kg_ref[...]).astype(BF16)
            kn_ref[pl.ds(dst, kb), :] = jnp.concatenate([jnp.where(first, kn, zero_b), cpart], axis=1)
            kn_ref[pl.ds(dst + kb, kb), :] = jnp.concatenate([jnp.where(first, zero_b, kn), cpart], axis=1)
            return carry

        lax.fori_loop(0, nk, prep, 0)

    qn = (halfnorm(q_ref[...].astype(F32), qg_ref[...]) * (scale * LOG2E)).astype(BF16)
    sw_l = jnp.broadcast_to(jnp.where(lane < 2, 1.0, 0.0), (tq, LANES)).astype(BF16)
    sw_r = jnp.broadcast_to(jnp.where((lane >= 2) & (lane < 4), 1.0, 0.0), (tq, LANES)).astype(BF16)
    q_left = jnp.concatenate([qn, sw_l], axis=1)
    q_right = jnp.concatenate([qn, sw_r], axis=1)
    q_near = jnp.concatenate([qn, jnp.zeros((tq, LANES), BF16)], axis=1)

    def score_tile(t, qz, bias):
        off = pl.multiple_of(t * 2 * kb, 2 * kb)
        s12 = _dot_nt(qz, kn_ref[pl.ds(off, 2 * kb), :])
        s1 = s12[:, :kb]
        s2 = s12[:, kb:]
        if bias is not None:
            s1 = s1 + bias
            s2 = s2 + bias
        s1_ref[t] = s1
        s2_ref[t] = s2
        return s1, s2

    near_lo = qi * (tq // kb) - 1
    n_left = jnp.maximum(near_lo, 0)
    near_hi = jnp.minimum(near_lo + n_near, nk)
    n_far = n_left + (nk - near_hi)

    def far_body(it, carry):
        m1, m2 = carry
        for u in range(ATTN_QK_UNROLL):
            f = jnp.minimum(it * ATTN_QK_UNROLL + u, n_far - 1)
            t = jnp.where(f < n_left, f, f - n_left + near_hi)
            s1, s2 = score_tile(t, jnp.where(f < n_left, q_left, q_right), None)
            m1 = jnp.maximum(m1, s1)
            m2 = jnp.maximum(m2, s2)
        return m1, m2

    minf = jnp.full((tq, LANES), -jnp.inf, F32)
    far_m1, far_m2 = lax.fori_loop(0, (n_far + ATTN_QK_UNROLL - 1) // ATTN_QK_UNROLL, far_body, (minf, minf))
    interior = (near_lo >= 0) & (near_lo + n_near <= nk)

    @pl.when(interior)
    def _():
        m1, m2 = far_m1, far_m2
        for j in range(n_near):
            s1, s2 = score_tile(near_lo + j, q_near, bias_ref[j])
            m1 = jnp.maximum(m1, s1)
            m2 = jnp.maximum(m2, s2)
        mx1_ref[...] = m1
        mx2_ref[...] = m2

    @pl.when(jnp.logical_not(interior))
    def _():
        mx1_ref[...] = far_m1
        mx2_ref[...] = far_m2
        for j in range(n_near):
            t = near_lo + j

            @pl.when((t >= 0) & (t < nk))
            def _():
                s1, s2 = score_tile(t, q_near, bias_ref[j])
                mx1_ref[...] = jnp.maximum(mx1_ref[...], s1)
                mx2_ref[...] = jnp.maximum(mx2_ref[...], s2)

    m1 = jnp.max(mx1_ref[...], axis=1, keepdims=True)
    m2 = jnp.max(mx2_ref[...], axis=1, keepdims=True)

    def pv_body(ch, carry):
        l1, l2, a1, a2 = carry
        p1s, p2s = [], []
        for u in range(ATTN_PV_TILES):
            t = ch * ATTN_PV_TILES + u
            p1 = jnp.exp2(s1_ref[t] - m1)
            p2 = jnp.exp2(s2_ref[t] - m2)
            l1 = l1 + p1
            l2 = l2 + p2
            p1s.append(p1.astype(BF16))
            p2s.append(p2.astype(BF16))
        vch = v_ref[pl.ds(pl.multiple_of(ch * ATTN_PV_TILES * kb, ATTN_PV_TILES * kb), ATTN_PV_TILES * kb), :]
        a1 = a1 + _dot(jnp.concatenate(p1s, axis=1), vch)
        a2 = a2 + _dot(jnp.concatenate(p2s, axis=1), vch)
        return l1, l2, a1, a2

    z = jnp.zeros((tq, LANES), F32)
    l1, l2, a1, a2 = lax.fori_loop(0, nk // ATTN_PV_TILES, pv_body, (z, z, z, z), unroll=2)
    l1 = jnp.sum(l1, axis=1, keepdims=True)
    l2 = jnp.sum(l2, axis=1, keepdims=True)
    o = a1 / l1 - lam_ref[...] * (a2 / l2)
    ms = jnp.mean(o * o, axis=-1, keepdims=True)
    o_ref[...] = (o * lax.rsqrt(ms + EPS) * sg_ref[...] * out_scale).astype(BF16)


def _attention(proj3, bias_near, cfar, qg, kg, sg, lam_row, lam_init):
    B, S, _ = proj3.shape
    tq = ATTN_TQ
    nq = S // tq
    nk = S // ATTN_KB
    n_near = tq // ATTN_KB + 2
    return pl.pallas_call(
        functools.partial(_attn_kernel, nk=nk, scale=ATTN_HEAD_DIM ** -0.5, out_scale=1.0 - lam_init),
        grid=(B, ATTN_HEADS, nq),
        in_specs=[
            pl.BlockSpec((None, tq, LANES), lambda b, h, q: (b, q, COL_Q // LANES + h)),
            pl.BlockSpec((None, S, LANES), lambda b, h, q: (b, 0, COL_K // LANES + h)),
            pl.BlockSpec((None, S, LANES), lambda b, h, q: (b, 0, COL_V // LANES + h)),
            pl.BlockSpec((None, n_near, tq, LANES), lambda b, h, q: (h, 0, 0, 0)),
            pl.BlockSpec((None, SUBLANES, LANES), lambda b, h, q: (h, 0, 0)),
            pl.BlockSpec((1, LANES), lambda b, h, q: (0, 0)),
            pl.BlockSpec((1, LANES), lambda b, h, q: (0, 0)),
            pl.BlockSpec((1, LANES), lambda b, h, q: (0, 0)),
            pl.BlockSpec((1, LANES), lambda b, h, q: (0, 0)),
        ],
        out_specs=pl.BlockSpec((None, tq, LANES), lambda b, h, q: (b, q, h)),
        out_shape=jax.ShapeDtypeStruct((B, S, ATTN_HEADS * ATTN_V_DIM), BF16),
        scratch_shapes=[
            pltpu.VMEM((2 * S, 2 * LANES), BF16),
            pltpu.VMEM((nk, tq, LANES), F32),
            pltpu.VMEM((nk, tq, LANES), F32),
            pltpu.VMEM((tq, LANES), F32),
            pltpu.VMEM((tq, LANES), F32),
        ],
        compiler_params=_cparams(("parallel", "parallel", "arbitrary")),
        name="diff_attn",
    )(proj3, proj3, proj3, bias_near, cfar, qg, kg, sg, lam_row)


def _mix_kernel(x_ref, yf_ref, yb_ref, xbc_ref, z_ref, at_ref, gs_ref, ga_ref, dsk_ref, ng_ref,
                wssd_ref, wattn_ref, wout_ref, h_ref):
    xs = xbc_ref[...][:, :SSD_D_INNER].astype(F32)
    y = yf_ref[...] + yb_ref[...] + xs * dsk_ref[...]
    z = z_ref[...].astype(F32)
    y = y * (z * jax.nn.sigmoid(z))
    gw = SSD_D_INNER // SSD_N_GROUPS
    parts = []
    for g in range(SSD_N_GROUPS):
        yg = y[:, g * gw:(g + 1) * gw]
        parts.append(yg * lax.rsqrt(jnp.mean(yg * yg, axis=-1, keepdims=True) + EPS))
    yn = (jnp.concatenate(parts, axis=1) * ng_ref[...]).astype(BF16)
    y_ssd = _dot(yn, wssd_ref[...])
    y_attn = _dot(at_ref[...], wattn_ref[...])
    mixed = (jax.nn.sigmoid(gs_ref[...].astype(F32)) * y_ssd
             + jax.nn.sigmoid(ga_ref[...].astype(F32)) * y_attn)
    h_ref[...] = x_ref[...] + _dot(mixed.astype(BF16), wout_ref[...])


def _mix(x2, yf2, yb2, xbc2, proj2, attn2, dsk_row, ng_row, wssd, wattn, wout, tm):
    T = x2.shape[0]
    row = lambda i: (i, 0)
    const = lambda i: (0, 0)
    return pl.pallas_call(
        _mix_kernel,
        grid=(T // tm,),
        in_specs=[
            pl.BlockSpec((tm, D_MODEL), row),
            pl.BlockSpec((tm, SSD_D_INNER), row),
            pl.BlockSpec((tm, SSD_D_INNER), row),
            pl.BlockSpec((tm, SSD_CONV_DIM), row),
            pl.BlockSpec((tm, D_MODEL), lambda i: (i, COL_Z // D_MODEL)),
            pl.BlockSpec((tm, D_MODEL), row),
            pl.BlockSpec((tm, D_MODEL), lambda i: (i, COL_GS // D_MODEL)),
            pl.BlockSpec((tm, D_MODEL), lambda i: (i, COL_GA // D_MODEL)),
            pl.BlockSpec((1, D_MODEL), const),
            pl.BlockSpec((1, D_MODEL), const),
            pl.BlockSpec((D_MODEL, D_MODEL), const),
            pl.BlockSpec((D_MODEL, D_MODEL), const),
            pl.BlockSpec((D_MODEL, D_MODEL), const),
        ],
        out_specs=pl.BlockSpec((tm, D_MODEL), row),
        out_shape=jax.ShapeDtypeStruct((T, D_MODEL), F32),
        compiler_params=_cparams(("parallel",)),
        name="mix",
    )(x2, yf2, yb2, xbc2, proj2, attn2, proj2, proj2, dsk_row, ng_row, wssd, wattn, wout)


def _extract_topk_pass(work, n, break_ties):
    iota = lax.broadcasted_iota(jnp.int32, work.shape, 0)
    big = jnp.int32(work.shape[0])
    rank = jnp.full(work.shape, n, jnp.int32)
    vals = []
    for r in range(n):
        m = jnp.max(work, axis=0, keepdims=True)
        sel = work == m
        if break_ties:
            sel = iota == jnp.min(jnp.where(sel, iota, big), axis=0, keepdims=True)
        rank = jnp.where(sel, r, rank)
        work = jnp.where(sel, -jnp.inf, work)
        vals.append(m)
    return rank, tuple(vals)


def _extract_topk(work, n):
    rank, vals = _extract_topk_pass(work, n, False)
    count = jnp.sum(jnp.where(rank < n, 1.0, 0.0), axis=0, keepdims=True)
    tied = jnp.max(count) > n
    return lax.cond(tied, lambda: _extract_topk_pass(work, n, True), lambda: (rank, vals))


def _peer_kernel(h_ref, g2_ref, wqT_ref, sk_ref, u_ref, vT_ref, o_ref,
                 hnT_ref, qT_ref, rb_ref, eb_ref, ni_ref, fi_ref, acc_ref, cand_ref, hid_a_ref, hid_b_ref):
    i = pl.program_id(1)
    K = PEER_TOPK
    tb = h_ref.shape[0]
    gw = qT_ref.shape[2]
    n_groups = tb // gw
    half = PEER_N_KEYS // 2

    @pl.when(i == 0)
    def _router():
        h = h_ref[...]
        hn = h * lax.rsqrt(jnp.mean(h * h, axis=-1, keepdims=True) + EPS) * g2_ref[...]
        hnT = hn.T.astype(BF16)
        hnT_ref[...] = hnT
        qT = _dot(wqT_ref[...], hnT).astype(BF16)
        for g in range(n_groups):
            qT_ref[g] = qT[:, g * gw:(g + 1) * gw]
        acc_ref[...] = jnp.zeros_like(acc_ref)
        hid_b_ref[...] = jnp.zeros_like(hid_b_ref)
        cand_ref[...] = jnp.full(cand_ref.shape, -jnp.inf, F32)

        def route(x, carry):
            hd = x // n_groups
            g = x % n_groups
            ranks, tops, scores = [], [], []
            for m in range(2):
                off = pl.multiple_of(hd * PEER_N_KEYS + m * half, half)
                s = _dot(sk_ref[m], qT_ref[g, pl.ds(off, half), :])
                rank, vals = _extract_topk(s, K)
                ranks.append(rank)
                tops.append(vals)
                scores.append(s)
            for r, (a, b) in enumerate(_CAND):
                cand_ref[r:r + 1, :] = tops[0][a] + tops[1][b]
            cand = cand_ref[...]
            crank, cvals = _extract_topk(cand, K)
            csel = crank < K
            zsum = jnp.sum(jnp.where(csel, jnp.exp(cand - cvals[0]), 0.0), axis=0, keepdims=True)
            crow = lax.broadcasted_iota(jnp.int32, cand.shape, 0)
            self_f = csel.astype(F32)
            ni = jnp.zeros((PEER_N_KEYS, gw), F32)
            r0 = 0
            for a in range(K):
                r1 = r0 + K // (a + 1)
                n_a = jnp.sum(jnp.where((crow >= r0) & (crow < r1), self_f, 0.0), axis=0, keepdims=True)
                ni = ni + jnp.where(ranks[0] == a, n_a, 0.0)
                r0 = r1
            ni_ref[hd, g] = ni
            fi_ref[hd, g] = jnp.where(ranks[0] < K, jnp.exp(scores[0] - tops[0][0]), 0.0)
            rb_ref[hd, g] = ranks[1].astype(F32).astype(BF16)
            eb_ref[hd, g] = jnp.where(ranks[1] < K, jnp.exp(scores[1] - tops[1][0]) / zsum, 0.0).astype(BF16)
            return carry

        lax.fori_loop(0, PEER_HEADS * n_groups, route, 0)

    def lanes(ref, hd, rows):
        return jnp.concatenate([ref[hd, g, rows, :] for g in range(n_groups)], axis=1)

    def dense_step(read_ref, write_ref):
        hid_new = _dot(u_ref[...], hnT_ref[...])
        prev = jnp.maximum(i - 1, 0)
        ws = []
        for it in range(PEER_STEP_TILES):
            key0 = prev * PEER_STEP_TILES + it
            hid = read_ref[it * PEER_N_KEYS:(it + 1) * PEER_N_KEYS, :]
            ge = hid * (1.0 + lax.erf(hid))
            gate = jnp.zeros(hid.shape, BF16)
            for hd in range(PEER_HEADS):
                n_row = lanes(ni_ref, hd, pl.ds(key0, 1)).astype(BF16)
                f_row = lanes(fi_ref, hd, pl.ds(key0, 1)).astype(BF16)
                sel = lanes(rb_ref, hd, slice(None)) < n_row
                gate = gate + jnp.where(sel, lanes(eb_ref, hd, slice(None)) * f_row, jnp.zeros((), BF16))
            ws.append((ge * gate.astype(F32)).astype(BF16))
        kw = 2 * PEER_N_KEYS
        upd = _dot(vT_ref[:, :kw], jnp.concatenate(ws[:2], axis=0))
        for p in range(1, PEER_STEP_TILES // 2):
            upd = upd + _dot(vT_ref[:, p * kw:(p + 1) * kw], jnp.concatenate(ws[2 * p:2 * p + 2], axis=0))
        acc_ref[...] += upd
        write_ref[...] = hid_new

    @pl.when(i % 2 == 0)
    def _():
        dense_step(hid_b_ref, hid_a_ref)

    @pl.when(i % 2 == 1)
    def _():
        dense_step(hid_a_ref, hid_b_ref)

    @pl.when(i == pl.num_programs(1) - 1)
    def _():
        o_ref[...] = h_ref[...] + acc_ref[...].T


PEER_STEP_TILES = 4
PEER_GROUP = 512


def _peer(h2, g2_row, wqT, sk, u_b, vT_b, tb):
    T = h2.shape[0]
    nk = PEER_N_KEYS
    ne = nk * PEER_STEP_TILES
    n_blocks = nk // PEER_STEP_TILES
    gw = min(tb, PEER_GROUP)
    tile = (PEER_HEADS, tb // gw, nk, gw)
    return pl.pallas_call(
        _peer_kernel,
        grid=(T // tb, n_blocks + 1),
        in_specs=[
            pl.BlockSpec((tb, D_MODEL), lambda t, i: (t, 0)),
            pl.BlockSpec((1, D_MODEL), lambda t, i: (0, 0)),
            pl.BlockSpec((D_MODEL, D_MODEL), lambda t, i: (0, 0)),
            pl.BlockSpec((2, nk, nk // 2), lambda t, i: (0, 0, 0)),
            pl.BlockSpec((ne, D_MODEL), lambda t, i: (jnp.minimum(i, n_blocks - 1), 0)),
            pl.BlockSpec((D_MODEL, ne), lambda t, i: (0, jnp.maximum(i - 1, 0))),
        ],
        out_specs=pl.BlockSpec((tb, D_MODEL), lambda t, i: (t, 0)),
        out_shape=jax.ShapeDtypeStruct((T, D_MODEL), F32),
        scratch_shapes=[
            pltpu.VMEM((D_MODEL, tb), BF16),
            pltpu.VMEM((tb // gw, D_MODEL, gw), BF16),
            pltpu.VMEM(tile, BF16),
            pltpu.VMEM(tile, BF16),
            pltpu.VMEM(tile, F32),
            pltpu.VMEM(tile, F32),
            pltpu.VMEM((D_MODEL, tb), F32),
            pltpu.VMEM((_CAND_ROWS, gw), F32),
            pltpu.VMEM((ne, tb), F32),
            pltpu.VMEM((ne, tb), F32),
        ],
        compiler_params=_cparams(("parallel", "arbitrary")),
        name="peer",
    )(h2, g2_row, wqT, sk, u_b, vT_b)


def _rel_bucket(rel):
    nb = REL_BUCKETS // 2
    ret = jnp.where(rel > 0, nb, 0)
    n = jnp.abs(rel)
    max_exact = nb // 2
    nf = jnp.maximum(n, 1).astype(F32)
    large = max_exact + (jnp.log(nf / max_exact) / math.log(REL_MAX_DIST / max_exact)
                         * (nb - max_exact)).astype(jnp.int32)
    large = jnp.minimum(large, nb - 1)
    return ret + jnp.where(n < max_exact, n, large)


def _bias_tiles(rel_bias):
    tq, kb = ATTN_TQ, ATTN_KB
    n_near = tq // kb + 2
    width = n_near * kb
    period = width + tq
    x = jnp.arange(period, dtype=jnp.int32)
    rel = jnp.where(x < width, x, x - period) - kb
    table = (rel_bias[_rel_bucket(rel)].astype(F32) * LOG2E).T
    skew = jnp.broadcast_to(table[:, None, :], (ATTN_HEADS, tq, period)).reshape(ATTN_HEADS, tq * period)
    skew = skew[:, :tq * (period - 1)].reshape(ATTN_HEADS, tq, period - 1)
    near = skew[:, :, :width].reshape(ATTN_HEADS, tq, n_near, kb).transpose(0, 2, 1, 3)
    far = rel_bias[_rel_bucket(jnp.array([-REL_MAX_DIST, REL_MAX_DIST], jnp.int32))].astype(F32).T * LOG2E
    cfar = jnp.broadcast_to(jnp.pad(far, ((0, 0), (0, SUBLANES - 2)))[:, :, None], (ATTN_HEADS, SUBLANES, LANES))
    return near, cfar


def _pad_rows(a, n):
    return jnp.pad(a, ((0, n - a.shape[0]), (0, 0)))


def _prep(rel_bias, norm1_gain, w_in, conv_w, conv_b, dt_bias_f, dt_bias_b, a_log_f, a_log_b, d_skip,
          ssd_norm_gain, w_ssd_proj, q_norm_gain, k_norm_gain, lambda_q1, lambda_k1, lambda_q2, lambda_k2,
          subln_gain, w_attn_proj, w_out, norm2_gain, peer_w_q, peer_sub_keys, peer_u, peer_v, lam_init):
    o1 = SSD_D_INNER
    o2 = o1 + SSD_CONV_DIM
    o3 = o2 + SSD_N_HEADS
    o4 = o3 + SSD_N_HEADS
    o5 = o4 + 1024
    o6 = o5 + 1024
    o7 = o6 + 1024
    o8 = o7 + D_MODEL
    w = w_in
    zeros = jnp.zeros((D_MODEL, COL_XBC - (COL_GA + D_MODEL)), F32)
    slab = jnp.concatenate([w[:, :o1], w[:, o4:o5], w[:, o5:o6], w[:, o6:o7], w[:, o7:o8], w[:, o8:],
                            zeros, w[:, o1:o2]], axis=1).astype(BF16)
    w_dt = jnp.pad(w[:, o2:o4], ((0, 0), (0, LANES - 2 * SSD_N_HEADS))).astype(BF16)
    row128 = lambda f, b: jnp.pad(jnp.concatenate([f, b]), (0, LANES - 2 * SSD_N_HEADS)).reshape(1, LANES)
    hp = jnp.arange(SSD_D_INNER, dtype=jnp.int32) // SSD_HEAD_DIM
    rows = jnp.arange(LANES, dtype=jnp.int32)[:, None]
    f32 = F32
    lam = (jnp.exp(jnp.sum(lambda_q1.astype(f32) * lambda_k1.astype(f32)))
           - jnp.exp(jnp.sum(lambda_q2.astype(f32) * lambda_k2.astype(f32))) + lam_init)
    return dict(
        gain1=norm1_gain.reshape(1, D_MODEL),
        slab=slab,
        w_dt=w_dt,
        cw=_pad_rows(conv_w.reshape(SSD_CONV, SSD_CONV_DIM), SUBLANES),
        cb=conv_b.reshape(1, SSD_CONV_DIM),
        dtb=row128(dt_bias_f, dt_bias_b),
        alog=row128(a_log_f, a_log_b),
        ex_f=(rows == hp[None, :]).astype(BF16),
        ex_b=(rows == hp[None, :] + SSD_N_HEADS).astype(BF16),
        dsk=jnp.repeat(d_skip, SSD_HEAD_DIM).reshape(1, SSD_D_INNER),
        ng=ssd_norm_gain.reshape(1, SSD_D_INNER),
        wssd=w_ssd_proj.astype(BF16),
        wattn=w_attn_proj.astype(BF16),
        wout=w_out.astype(BF16),
        bias=_bias_tiles(rel_bias),
        qg=jnp.tile(q_norm_gain, 2).reshape(1, LANES),
        kg=jnp.tile(k_norm_gain, 2).reshape(1, LANES),
        sg=subln_gain.reshape(1, LANES),
        lam=jnp.broadcast_to(lam, (1, LANES)).astype(F32),
        g2=norm2_gain.reshape(1, D_MODEL),
        wqT=peer_w_q.T.astype(BF16),
        sk=peer_sub_keys.astype(BF16),
        u=(peer_u * (2.0 ** -0.5)).astype(BF16),
        vT=(peer_v.T * (2.0 ** -0.5)).astype(BF16),
    )


def _largest_divisor(n, cap):
    t = cap
    while n % t:
        t //= 2
    return t


def _layer(x, p, lam_init):
    B, S, _ = x.shape
    T = B * S
    x2 = x.reshape(T, D_MODEL)
    tm = _largest_divisor(T, 1024)
    proj2, dt2 = _inproj(x2, p["gain1"], p["slab"], p["w_dt"], tm)
    proj3 = proj2.reshape(B, S, PROJ_W)
    dt3 = dt2.reshape(B, S, LANES)
    xbc3 = _conv(proj3, p["cw"], p["cb"])
    yf = _ssd_sweep(xbc3, dt3, p["dtb"], p["alog"], p["ex_f"], rev=False)
    yb = _ssd_sweep(xbc3, dt3, p["dtb"], p["alog"], p["ex_b"], rev=True)
    attn = _attention(proj3, *p["bias"], p["qg"], p["kg"], p["sg"], p["lam"], lam_init)
    h2 = _mix(x2, yf.reshape(T, -1), yb.reshape(T, -1), xbc3.reshape(T, -1), proj2, attn.reshape(T, -1),
              p["dsk"], p["ng"], p["wssd"], p["wattn"], p["wout"], _largest_divisor(T, 512))
    y2 = _peer(h2, p["g2"], p["wqT"], p["sk"], p["u"], p["vT"], _largest_divisor(T, 512))
    return y2.reshape(B, S, D_MODEL)


def kernel(x_prompt, x_sample, rel_bias, norm1_gain, w_in, conv_w, conv_b, dt_bias_f, dt_bias_b, a_log_f, a_log_b, d_skip, ssd_norm_gain, w_ssd_proj, q_norm_gain, k_norm_gain, lambda_q1, lambda_k1, lambda_q2, lambda_k2, subln_gain, w_attn_proj, w_out, norm2_gain, peer_w_q, peer_sub_keys, peer_u, peer_v):
    depth = norm1_gain.shape[0]
    lam_inits = [0.8 - 0.6 * math.exp(-0.3 * i) for i in range(depth)]
    preps = [
        _prep(rel_bias, norm1_gain[i], w_in[i], conv_w[i], conv_b[i], dt_bias_f[i], dt_bias_b[i],
              a_log_f[i], a_log_b[i], d_skip[i], ssd_norm_gain[i], w_ssd_proj[i], q_norm_gain[i],
              k_norm_gain[i], lambda_q1[i], lambda_k1[i], lambda_q2[i], lambda_k2[i], subln_gain[i],
              w_attn_proj[i], w_out[i], norm2_gain[i], peer_w_q[i], peer_sub_keys[i], peer_u[i],
              peer_v[i], lam_inits[i])
        for i in range(depth)
    ]
    outs = []
    for x in (x_prompt, x_sample):
        for p, lam_init in zip(preps, lam_inits):
            x = _layer(x, p, lam_init)
        outs.append(x)
    return tuple(outs)
```

```python
import functools
import math

import jax
import jax.numpy as jnp
from jax import lax
from jax.experimental import pallas as pl
from jax.experimental.pallas import tpu as pltpu

F32 = jnp.float32
BF16 = jnp.bfloat16

D_MODEL = 1024
SSD_D_INNER = 1024
SSD_HEAD_DIM = 64
SSD_N_HEADS = 16
SSD_N_GROUPS = 2
SSD_D_STATE = 64
SSD_CONV = 5
SSD_CHUNK = 128
SSD_CONV_DIM = 1280
ATTN_HEADS = 8
ATTN_HEAD_DIM = 64
ATTN_V_DIM = 128
REL_BUCKETS = 32
REL_MAX_DIST = 128
PEER_N_KEYS = 128
PEER_HEADS = 8
PEER_TOPK = 16
EPS = 1e-6

LANES = 128
SUBLANES = 8
VMEM_LIMIT = 56 * 1024 * 1024

COL_Z = 0
COL_Q = 1024
COL_K = 2048
COL_V = 3072
COL_GS = 4096
COL_GA = 5120
COL_XBC = 6400
PROJ_W = 7680
PROJ_TN = 1536

_CAND = [(a, b) for a in range(PEER_TOPK) for b in range(PEER_TOPK) if (a + 1) * (b + 1) <= PEER_TOPK]
_CAND_ROWS = 56


def _dot(a, b):
    return jnp.dot(a, b, preferred_element_type=F32)


def _dot_nt(a, b):
    return lax.dot_general(a, b, (((1,), (1,)), ((), ())), preferred_element_type=F32)


def _split3(v):
    hi = v.astype(BF16)
    r = v - hi.astype(F32)
    mid = r.astype(BF16)
    lo = (r - mid.astype(F32)).astype(BF16)
    return hi, mid, lo


def _exact_dot_rhs(v, m):
    hi, mid, lo = _split3(v)
    return _dot(hi, m) + _dot(mid, m) + _dot(lo, m)


def _exact_dot_lhs(m, v):
    hi, mid, lo = _split3(v)
    return _dot(m, hi) + _dot(m, mid) + _dot(m, lo)


def _cparams(sem, flags=None):
    return pltpu.CompilerParams(dimension_semantics=sem, vmem_limit_bytes=VMEM_LIMIT, flags=flags)


def _inproj_kernel(x_ref, g_ref, w_ref, wdt_ref, o_ref, dt_ref, xn_ref):
    j = pl.program_id(1)

    @pl.when(j == 0)
    def _():
        x = x_ref[...]
        ms = jnp.mean(x * x, axis=-1, keepdims=True)
        xn = (x * lax.rsqrt(ms + EPS) * g_ref[...]).astype(BF16)
        xn_ref[...] = xn
        dt_ref[...] = _dot(xn, wdt_ref[...])

    o_ref[...] = _dot(xn_ref[...], w_ref[...]).astype(BF16)


def _inproj(x2, gain, w_slab, w_dt, tm):
    T = x2.shape[0]
    grid = (T // tm, PROJ_W // PROJ_TN)
    return pl.pallas_call(
        _inproj_kernel,
        grid=grid,
        in_specs=[
            pl.BlockSpec((tm, D_MODEL), lambda i, j: (i, 0)),
            pl.BlockSpec((1, D_MODEL), lambda i, j: (0, 0)),
            pl.BlockSpec((D_MODEL, PROJ_TN), lambda i, j: (0, j)),
            pl.BlockSpec((D_MODEL, LANES), lambda i, j: (0, 0)),
        ],
        out_specs=[
            pl.BlockSpec((tm, PROJ_TN), lambda i, j: (i, j)),
            pl.BlockSpec((tm, LANES), lambda i, j: (i, 0)),
        ],
        out_shape=[
            jax.ShapeDtypeStruct((T, PROJ_W), BF16),
            jax.ShapeDtypeStruct((T, LANES), F32),
        ],
        scratch_shapes=[pltpu.VMEM((tm, D_MODEL), BF16)],
        compiler_params=_cparams(("parallel", "arbitrary")),
        name="inproj",
    )(x2, gain, w_slab, w_dt)


def _conv_kernel(xp_ref, xc_ref, xn_ref, cw_ref, cb_ref, o_ref, pad_ref, *, nc):
    c = pl.program_id(1)
    L = SSD_CHUNK
    H = SUBLANES
    xp = jnp.where(c > 0, xp_ref[...].astype(F32), 0.0)
    xn = jnp.where(c < nc - 1, xn_ref[...].astype(F32), 0.0)
    pad_ref[0:H, :] = xp
    pad_ref[H:H + L, :] = xc_ref[...].astype(F32)
    pad_ref[H + L:2 * H + L, :] = xn
    half = (SSD_CONV - 1) // 2
    acc = jnp.broadcast_to(cb_ref[...], (L, SSD_CONV_DIM))
    for k in range(SSD_CONV):
        off = H - half + k
        acc = acc + cw_ref[k:k + 1, :] * pad_ref[off:off + L, :]
    o_ref[...] = (acc * jax.nn.sigmoid(acc)).astype(BF16)


def _conv(proj3, cw, cb):
    B, S, _ = proj3.shape
    L = SSD_CHUNK
    nc = S // L
    rb = L // SUBLANES
    cblk = COL_XBC // SSD_CONV_DIM
    return pl.pallas_call(
        functools.partial(_conv_kernel, nc=nc),
        grid=(B, nc),
        in_specs=[
            pl.BlockSpec((None, SUBLANES, SSD_CONV_DIM), lambda b, c: (b, jnp.maximum(c * rb - 1, 0), cblk)),
            pl.BlockSpec((None, L, SSD_CONV_DIM), lambda b, c: (b, c, cblk)),
            pl.BlockSpec((None, SUBLANES, SSD_CONV_DIM),
                         lambda b, c: (b, jnp.minimum((c + 1) * rb, S // SUBLANES - 1), cblk)),
            pl.BlockSpec((SUBLANES, SSD_CONV_DIM), lambda b, c: (0, 0)),
            pl.BlockSpec((1, SSD_CONV_DIM), lambda b, c: (0, 0)),
        ],
        out_specs=pl.BlockSpec((None, L, SSD_CONV_DIM), lambda b, c: (b, c, 0)),
        out_shape=jax.ShapeDtypeStruct((B, S, SSD_CONV_DIM), BF16),
        scratch_shapes=[pltpu.VMEM((L + 2 * SUBLANES, SSD_CONV_DIM), F32)],
        compiler_params=_cparams(("parallel", "parallel")),
        name="ssd_conv",
    )(proj3, proj3, proj3, cw, cb)


def _ssd_sweep_kernel(xbc_ref, dt_ref, dtb_ref, alog_ref, ex_ref, y_ref, state_ref, *, rev, hoff):
    c = pl.program_id(1)
    L = SSD_CHUNK

    @pl.when(c == 0)
    def _():
        state_ref[...] = jnp.zeros_like(state_ref)

    xbc = xbc_ref[...]
    xs = xbc[:, :SSD_D_INNER].astype(F32)
    Bm = xbc[:, SSD_D_INNER:SSD_D_INNER + LANES]
    Cm = xbc[:, SSD_D_INNER + LANES:SSD_D_INNER + 2 * LANES]
    ex = ex_ref[...]

    dt = jax.nn.softplus(dt_ref[...] + dtb_ref[...])
    a = dt * (-jnp.exp(alog_ref[...]))
    row = lax.broadcasted_iota(jnp.int32, (L, L), 0)
    col = lax.broadcasted_iota(jnp.int32, (L, L), 1)
    tri = (col >= row) if rev else (col <= row)
    cs = _exact_dot_lhs(tri.astype(BF16), a)
    csT = cs.T
    end = 0 if rev else L - 1
    cs_end = cs[end:end + 1, :]

    xdt = xs * _exact_dot_rhs(dt, ex)
    xdt_b = xdt.astype(BF16)
    xw = (xdt * _exact_dot_rhs(jnp.exp(cs_end - cs), ex)).astype(BF16)

    lane = lax.broadcasted_iota(jnp.int32, (1, LANES), 1)
    g0 = lane < SSD_D_STATE
    zero_b = jnp.zeros((), BF16)
    cb0 = _dot_nt(jnp.where(g0, Cm, zero_b), Bm)
    cb1 = _dot_nt(jnp.where(g0, zero_b, Cm), Bm)

    ys = []
    for pair in range(SSD_N_HEADS // 2):
        ms = []
        for hh in (2 * pair, 2 * pair + 1):
            k = hoff + hh
            seg = cs[:, k:k + 1] - csT[k:k + 1, :]
            dec = jnp.exp(jnp.where(tri, seg, -jnp.inf))
            cbg = cb0 if hh < SSD_N_HEADS // 2 else cb1
            ms.append((cbg * dec).astype(BF16))
        lhs = jnp.concatenate(ms, axis=1)
        xp = xdt_b[:, LANES * pair:LANES * (pair + 1)]
        rhs = jnp.concatenate([jnp.where(g0, xp, zero_b), jnp.where(g0, zero_b, xp)], axis=0)
        ys.append(_dot(lhs, rhs))
    y_diag = jnp.concatenate(ys, axis=1)

    st = state_ref[...]
    y_off = _dot(Cm, st.astype(BF16)) * _exact_dot_rhs(jnp.exp(cs), ex)
    y_ref[...] = y_diag + y_off

    dec_c = _exact_dot_rhs(jnp.broadcast_to(jnp.exp(cs_end), (SUBLANES, LANES)), ex)[0:1, :]
    BmT = Bm.astype(F32).T.astype(BF16)
    upd = _dot(BmT, xw)
    r2 = lax.broadcasted_iota(jnp.int32, (LANES, SSD_D_INNER), 0) // SSD_D_STATE
    c2 = lax.broadcasted_iota(jnp.int32, (LANES, SSD_D_INNER), 1) // (SSD_D_INNER // SSD_N_GROUPS)
    state_ref[...] = st * dec_c + jnp.where(r2 == c2, upd, 0.0)


def _ssd_sweep(xbc3, dt3, dtb_row, alog_row, ex, rev):
    B, S, _ = xbc3.shape
    L = SSD_CHUNK
    nc = S // L
    hoff = SSD_N_HEADS if rev else 0
    if rev:
        cmap = lambda b, c: (b, nc - 1 - c, 0)
    else:
        cmap = lambda b, c: (b, c, 0)
    return pl.pallas_call(
        functools.partial(_ssd_sweep_kernel, rev=rev, hoff=hoff),
        grid=(B, nc),
        in_specs=[
            pl.BlockSpec((None, L, SSD_CONV_DIM), cmap),
            pl.BlockSpec((None, L, LANES), cmap),
            pl.BlockSpec((1, LANES), lambda b, c: (0, 0)),
            pl.BlockSpec((1, LANES), lambda b, c: (0, 0)),
            pl.BlockSpec((LANES, SSD_D_INNER), lambda b, c: (0, 0)),
        ],
        out_specs=pl.BlockSpec((None, L, SSD_D_INNER), cmap),
        out_shape=jax.ShapeDtypeStruct((B, S, SSD_D_INNER), F32),
        scratch_shapes=[pltpu.VMEM((LANES, SSD_D_INNER), F32)],
        compiler_params=_cparams(("parallel", "arbitrary")),
        name="ssd_sweep_rev" if rev else "ssd_sweep_fwd",
    )(xbc3, dt3, dtb_row, alog_row, ex)


ATTN_TQ_MAX = 512
ATTN_SCORE_BYTES = 16 * 1024 * 1024
ATTN_KB = LANES
ATTN_PV_TILES = 4
ATTN_QK_UNROLL = 6
LOG2E = math.log2(math.e)


def _attn_kernel(q_ref, k_ref, v_ref, bias_ref, cfar_ref, qg_ref, kg_ref, sg_ref, lam_ref, o_ref,
                 kn_ref, s1_ref, s2_ref, mx1_ref, mx2_ref, *, nk, scale, out_scale):
    qi = pl.program_id(2)
    tq = q_ref.shape[0]
    kb = ATTN_KB
    n_near = tq // kb + 2
    r = lax.broadcasted_iota(jnp.int32, (LANES, LANES), 0) // ATTN_HEAD_DIM
    c = lax.broadcasted_iota(jnp.int32, (LANES, LANES), 1) // ATTN_HEAD_DIM
    ones2 = (r == c).astype(BF16)
    lane = lax.broadcasted_iota(jnp.int32, (1, LANES), 1)
    first = lane < ATTN_HEAD_DIM
    zero_b = jnp.zeros((), BF16)

    def halfnorm(x, g):
        ms = _exact_dot_rhs(x * x, ones2) * (1.0 / ATTN_HEAD_DIM)
        return x * lax.rsqrt(ms + EPS) * g

    @pl.when(qi == 0)
    def _():
        cl = cfar_ref[0:1, :]
        cr = cfar_ref[1:2, :]
        cl_hi = cl.astype(BF16).astype(F32)
        cr_hi = cr.astype(BF16).astype(F32)
        crow = jnp.where(lane == 0, cl_hi, jnp.where(lane == 1, cl - cl_hi,
                         jnp.where(lane == 2, cr_hi, jnp.where(lane == 3, cr - cr_hi, 0.0))))
        cpart = jnp.broadcast_to(crow, (kb, LANES)).astype(BF16)

        def prep(t, carry):
            src = pl.multiple_of(t * kb, kb)
            dst = pl.multiple_of(t * 2 * kb, 2 * kb)
            kn = halfnorm(k_ref[pl.ds(src, kb), :].astype(F32), kg_ref[...]).astype(BF16)
            kn_ref[pl.ds(dst, kb), :] = jnp.concatenate([jnp.where(first, kn, zero_b), cpart], axis=1)
            kn_ref[pl.ds(dst + kb, kb), :] = jnp.concatenate([jnp.where(first, zero_b, kn), cpart], axis=1)
            return carry

        lax.fori_loop(0, nk, prep, 0, unroll=4)

    qn = (halfnorm(q_ref[...].astype(F32), qg_ref[...]) * (scale * LOG2E)).astype(BF16)
    sw_l = jnp.broadcast_to(jnp.where(lane < 2, 1.0, 0.0), (tq, LANES)).astype(BF16)
    sw_r = jnp.broadcast_to(jnp.where((lane >= 2) & (lane < 4), 1.0, 0.0), (tq, LANES)).astype(BF16)
    q_left = jnp.concatenate([qn, sw_l], axis=1)
    q_right = jnp.concatenate([qn, sw_r], axis=1)
    q_near = jnp.concatenate([qn, jnp.zeros((tq, LANES), BF16)], axis=1)

    def score_tile(t, qz, bias):
        off = pl.multiple_of(t * 2 * kb, 2 * kb)
        s12 = _dot_nt(qz, kn_ref[pl.ds(off, 2 * kb), :])
        s1 = s12[:, :kb]
        s2 = s12[:, kb:]
        if bias is not None:
            s1 = s1 + bias
            s2 = s2 + bias
        s1_ref[t] = s1
        s2_ref[t] = s2
        return s1, s2

    near_lo = qi * (tq // kb) - 1
    n_left = jnp.maximum(near_lo, 0)
    near_hi = jnp.minimum(near_lo + n_near, nk)
    n_far = n_left + (nk - near_hi)

    def far_body(it, carry):
        m1, m2 = carry
        for u in range(ATTN_QK_UNROLL):
            f = jnp.minimum(it * ATTN_QK_UNROLL + u, n_far - 1)
            t = jnp.where(f < n_left, f, f - n_left + near_hi)
            s1, s2 = score_tile(t, jnp.where(f < n_left, q_left, q_right), None)
            m1 = jnp.maximum(m1, s1)
            m2 = jnp.maximum(m2, s2)
        return m1, m2

    minf = jnp.full((tq, LANES), -jnp.inf, F32)
    far_m1, far_m2 = lax.fori_loop(0, (n_far + ATTN_QK_UNROLL - 1) // ATTN_QK_UNROLL, far_body, (minf, minf))
    interior = (near_lo >= 0) & (near_lo + n_near <= nk)

    @pl.when(interior)
    def _():
        m1, m2 = far_m1, far_m2
        for j in range(n_near):
            s1, s2 = score_tile(near_lo + j, q_near, bias_ref[j])
            m1 = jnp.maximum(m1, s1)
            m2 = jnp.maximum(m2, s2)
        mx1_ref[...] = m1
        mx2_ref[...] = m2

    @pl.when(jnp.logical_not(interior))
    def _():
        mx1_ref[...] = far_m1
        mx2_ref[...] = far_m2
        for j in range(n_near):
            t = near_lo + j

            @pl.when((t >= 0) & (t < nk))
            def _():
                s1, s2 = score_tile(t, q_near, bias_ref[j])
                mx1_ref[...] = jnp.maximum(mx1_ref[...], s1)
                mx2_ref[...] = jnp.maximum(mx2_ref[...], s2)

    m1 = jnp.max(mx1_ref[...], axis=1, keepdims=True)
    m2 = jnp.max(mx2_ref[...], axis=1, keepdims=True)

    def pv_body(ch, carry):
        l1, l2, a1, a2 = carry
        p1s, p2s = [], []
        for u in range(ATTN_PV_TILES):
            t = ch * ATTN_PV_TILES + u
            p1 = jnp.exp2(s1_ref[t] - m1)
            p2 = jnp.exp2(s2_ref[t] - m2)
            l1 = l1 + p1
            l2 = l2 + p2
            p1s.append(p1.astype(BF16))
            p2s.append(p2.astype(BF16))
        vch = v_ref[pl.ds(pl.multiple_of(ch * ATTN_PV_TILES * kb, ATTN_PV_TILES * kb), ATTN_PV_TILES * kb), :]
        a1 = a1 + _dot(jnp.concatenate(p1s, axis=1), vch)
        a2 = a2 + _dot(jnp.concatenate(p2s, axis=1), vch)
        return l1, l2, a1, a2

    z = jnp.zeros((tq, LANES), F32)
    l1, l2, a1, a2 = lax.fori_loop(0, nk // ATTN_PV_TILES, pv_body, (z, z, z, z), unroll=4)
    l1 = jnp.sum(l1, axis=1, keepdims=True)
    l2 = jnp.sum(l2, axis=1, keepdims=True)
    o = a1 / l1 - lam_ref[...] * (a2 / l2)
    ms = jnp.mean(o * o, axis=-1, keepdims=True)
    o_ref[...] = (o * lax.rsqrt(ms + EPS) * sg_ref[...] * out_scale).astype(BF16)


def _attn_tq(S):
    tq = ATTN_TQ_MAX
    while 2 * 4 * S * tq > ATTN_SCORE_BYTES or S % tq:
        tq //= 2
    return tq


def _attention(proj3, rel_bias, qg, kg, sg, lam_row, lam_init):
    B, S, _ = proj3.shape
    tq = _attn_tq(S)
    bias_near, cfar = _bias_tiles(rel_bias, tq)
    nq = S // tq
    nk = S // ATTN_KB
    n_near = tq // ATTN_KB + 2
    return pl.pallas_call(
        functools.partial(_attn_kernel, nk=nk, scale=ATTN_HEAD_DIM ** -0.5, out_scale=1.0 - lam_init),
        grid=(B, ATTN_HEADS, nq),
        in_specs=[
            pl.BlockSpec((None, tq, LANES), lambda b, h, q: (b, q, COL_Q // LANES + h)),
            pl.BlockSpec((None, S, LANES), lambda b, h, q: (b, 0, COL_K // LANES + h)),
            pl.BlockSpec((None, S, LANES), lambda b, h, q: (b, 0, COL_V // LANES + h)),
            pl.BlockSpec((None, n_near, tq, LANES), lambda b, h, q: (h, 0, 0, 0)),
            pl.BlockSpec((None, SUBLANES, LANES), lambda b, h, q: (h, 0, 0)),
            pl.BlockSpec((1, LANES), lambda b, h, q: (0, 0)),
            pl.BlockSpec((1, LANES), lambda b, h, q: (0, 0)),
            pl.BlockSpec((1, LANES), lambda b, h, q: (0, 0)),
            pl.BlockSpec((1, LANES), lambda b, h, q: (0, 0)),
        ],
        out_specs=pl.BlockSpec((None, tq, LANES), lambda b, h, q: (b, q, h)),
        out_shape=jax.ShapeDtypeStruct((B, S, ATTN_HEADS * ATTN_V_DIM), BF16),
        scratch_shapes=[
            pltpu.VMEM((2 * S, 2 * LANES), BF16),
            pltpu.VMEM((nk, tq, LANES), F32),
            pltpu.VMEM((nk, tq, LANES), F32),
            pltpu.VMEM((tq, LANES), F32),
            pltpu.VMEM((tq, LANES), F32),
        ],
        compiler_params=_cparams(("parallel", "parallel", "arbitrary")),
        name="diff_attn",
    )(proj3, proj3, proj3, bias_near, cfar, qg, kg, sg, lam_row)


def _mix_kernel(x_ref, yf_ref, yb_ref, xbc_ref, z_ref, at_ref, gs_ref, ga_ref, dsk_ref, ng_ref,
                wssd_ref, wattn_ref, wout_ref, h_ref):
    xs = xbc_ref[...][:, :SSD_D_INNER].astype(F32)
    y = yf_ref[...] + yb_ref[...] + xs * dsk_ref[...]
    z = z_ref[...].astype(F32)
    y = y * (z * jax.nn.sigmoid(z))
    gw = SSD_D_INNER // SSD_N_GROUPS
    parts = []
    for g in range(SSD_N_GROUPS):
        yg = y[:, g * gw:(g + 1) * gw]
        parts.append(yg * lax.rsqrt(jnp.mean(yg * yg, axis=-1, keepdims=True) + EPS))
    yn = (jnp.concatenate(parts, axis=1) * ng_ref[...]).astype(BF16)
    y_ssd = _dot(yn, wssd_ref[...])
    y_attn = _dot(at_ref[...], wattn_ref[...])
    mixed = (jax.nn.sigmoid(gs_ref[...].astype(F32)) * y_ssd
             + jax.nn.sigmoid(ga_ref[...].astype(F32)) * y_attn)
    h_ref[...] = x_ref[...] + _dot(mixed.astype(BF16), wout_ref[...])


def _mix(x2, yf2, yb2, xbc2, proj2, attn2, dsk_row, ng_row, wssd, wattn, wout, tm):
    T = x2.shape[0]
    row = lambda i: (i, 0)
    const = lambda i: (0, 0)
    return pl.pallas_call(
        _mix_kernel,
        grid=(T // tm,),
        in_specs=[
            pl.BlockSpec((tm, D_MODEL), row),
            pl.BlockSpec((tm, SSD_D_INNER), row),
            pl.BlockSpec((tm, SSD_D_INNER), row),
            pl.BlockSpec((tm, SSD_CONV_DIM), row),
            pl.BlockSpec((tm, D_MODEL), lambda i: (i, COL_Z // D_MODEL)),
            pl.BlockSpec((tm, D_MODEL), row),
            pl.BlockSpec((tm, D_MODEL), lambda i: (i, COL_GS // D_MODEL)),
            pl.BlockSpec((tm, D_MODEL), lambda i: (i, COL_GA // D_MODEL)),
            pl.BlockSpec((1, D_MODEL), const),
            pl.BlockSpec((1, D_MODEL), const),
            pl.BlockSpec((D_MODEL, D_MODEL), const),
            pl.BlockSpec((D_MODEL, D_MODEL), const),
            pl.BlockSpec((D_MODEL, D_MODEL), const),
        ],
        out_specs=pl.BlockSpec((tm, D_MODEL), row),
        out_shape=jax.ShapeDtypeStruct((T, D_MODEL), F32),
        compiler_params=_cparams(("parallel",)),
        name="mix",
    )(x2, yf2, yb2, xbc2, proj2, attn2, proj2, proj2, dsk_row, ng_row, wssd, wattn, wout)


def _extract_topk(work, n):
    iota = lax.broadcasted_iota(jnp.int32, work.shape, 0)
    big = jnp.int32(work.shape[0])
    rank = jnp.full(work.shape, n, jnp.int32)
    vals = []
    for r in range(n):
        m = jnp.max(work, axis=0, keepdims=True)
        sel = iota == jnp.min(jnp.where(work == m, iota, big), axis=0, keepdims=True)
        rank = jnp.where(sel, r, rank)
        work = jnp.where(sel, -jnp.inf, work)
        vals.append(m)
    return rank, vals


def _topk_values(s, n):
    m = jnp.max(s, axis=0, keepdims=True)
    vals = [m]
    for _ in range(1, n):
        m = jnp.max(jnp.where(s < m, s, -jnp.inf), axis=0, keepdims=True)
        vals.append(m)
    return vals


_ROUTE_ROWS = 16
_DENSE_ROWS = 32


def _count(mask):
    return jnp.sum(jnp.where(mask, 1.0, 0.0), axis=0, keepdims=True)


def _fill_candidates(cand_ref, top0, top1):
    for r, (a, b) in enumerate(_CAND):
        cand_ref[r:r + 1, :] = top0[a] + top1[b]
    return cand_ref[...]


def _route_exact(s0, s1, cand_ref):
    K = PEER_TOPK
    rank0, top0 = _extract_topk(s0, K)
    rank1, top1 = _extract_topk(s1, K)
    cand = _fill_candidates(cand_ref, top0, top1)
    crank, cvals = _extract_topk(cand, K)
    csel = crank < K
    zsum = jnp.sum(jnp.where(csel, jnp.exp(cand - cvals[0]), 0.0), axis=0, keepdims=True)
    crow = lax.broadcasted_iota(jnp.int32, cand.shape, 0)
    n_i = jnp.zeros(s0.shape, F32)
    r0 = 0
    for a in range(K):
        r1 = r0 + K // (a + 1)
        n_a = _count(csel & (crow >= r0) & (crow < r1))
        n_i = n_i + jnp.where(rank0 == a, n_a, 0.0)
        r0 = r1
    f_i = jnp.where(rank0 < K, jnp.exp(s0 - top0[0]), 0.0)
    e_b = jnp.where(rank1 < K, jnp.exp(s1 - top1[0]) / zsum, 0.0)
    return n_i, f_i, rank1.astype(F32), e_b


def _route_fast(s0, s1, cand_ref):
    K = PEER_TOPK
    top0 = _topk_values(s0, K)
    top1 = _topk_values(s1, K)
    cand = _fill_candidates(cand_ref, top0, top1)
    ctop = _topk_values(cand, K)
    tau = ctop[K - 1]
    in0 = s0 >= top0[K - 1]
    in1 = s1 >= top1[K - 1]
    n_parts, rank_parts = [], []
    for lo in range(0, PEER_N_KEYS, _ROUTE_ROWS):
        rows0 = s0[lo:lo + _ROUTE_ROWS]
        rows1 = s1[lo:lo + _ROUTE_ROWS]
        n_p = jnp.zeros(rows0.shape, F32)
        r_p = jnp.zeros(rows1.shape, F32)
        for r in range(K):
            n_p = n_p + jnp.where(rows0 + top1[r] >= tau, 1.0, 0.0)
            r_p = r_p + jnp.where(rows1 < top1[r], 1.0, 0.0)
        n_parts.append(n_p)
        rank_parts.append(r_p)
    n_i = jnp.where(in0, jnp.concatenate(n_parts, axis=0), 0.0)
    rank1 = jnp.concatenate(rank_parts, axis=0)
    zsum = jnp.sum(jnp.where(cand >= tau, jnp.exp(cand - ctop[0]), 0.0), axis=0, keepdims=True)
    f_i = jnp.where(in0, jnp.exp(s0 - top0[0]), 0.0)
    e_b = jnp.where(in1, jnp.exp(s1 - top1[0]) / zsum, 0.0)
    bad = (jnp.abs(_count(in0) - K) + jnp.abs(_count(in1) - K)
           + jnp.abs(jnp.sum(n_i, axis=0, keepdims=True) - K))
    return (n_i, f_i, rank1, e_b), jnp.max(bad) == 0.0


def _peer_kernel(h_ref, g2_ref, wqT_ref, sk_ref, u_ref, vT_ref, o_ref,
                 hnT_ref, qT_ref, rb_ref, eb_ref, ni_ref, fi_ref, acc_ref, cand_ref, hid_a_ref, hid_b_ref):
    i = pl.program_id(1)
    K = PEER_TOPK
    tb = h_ref.shape[0]
    gw = qT_ref.shape[2]
    n_groups = tb // gw
    half = PEER_N_KEYS // 2

    @pl.when(i == 0)
    def _router():
        h = h_ref[...]
        hn = h * lax.rsqrt(jnp.mean(h * h, axis=-1, keepdims=True) + EPS) * g2_ref[...]
        hnT = hn.T.astype(BF16)
        hnT_ref[...] = hnT
        qT = _dot(wqT_ref[...], hnT).astype(BF16)
        for g in range(n_groups):
            qT_ref[g] = qT[:, g * gw:(g + 1) * gw]
        acc_ref[...] = jnp.zeros_like(acc_ref)
        hid_b_ref[...] = jnp.zeros_like(hid_b_ref)
        cand_ref[...] = jnp.full(cand_ref.shape, -jnp.inf, F32)

        def route(x, carry):
            hd = x // n_groups
            g = x % n_groups
            scores = []
            for m in range(2):
                off = pl.multiple_of(hd * PEER_N_KEYS + m * half, half)
                scores.append(_dot(sk_ref[m], qT_ref[g, pl.ds(off, half), :]))
            fast, ok = _route_fast(scores[0], scores[1], cand_ref)
            n_i, f_i, rank1, e_b = lax.cond(ok, lambda: fast, lambda: _route_exact(scores[0], scores[1], cand_ref))
            ni_ref[hd, g] = n_i
            fi_ref[hd, g] = f_i
            rb_ref[hd, g] = rank1.astype(BF16)
            eb_ref[hd, g] = e_b.astype(BF16)
            return carry

        lax.fori_loop(0, PEER_HEADS * n_groups, route, 0)

    def lanes(ref, hd, rows):
        return jnp.concatenate([ref[hd, g, rows, :] for g in range(n_groups)], axis=1)

    def dense_step(read_ref, write_ref):
        hid_new = _dot(u_ref[...], hnT_ref[...])
        prev = jnp.maximum(i - 1, 0)
        ws = []
        for it in range(PEER_STEP_TILES):
            key0 = prev * PEER_STEP_TILES + it
            n_rows = [lanes(ni_ref, hd, pl.ds(key0, 1)).astype(BF16) for hd in range(PEER_HEADS)]
            f_rows = [lanes(fi_ref, hd, pl.ds(key0, 1)).astype(BF16) for hd in range(PEER_HEADS)]
            for lo in range(0, PEER_N_KEYS, _DENSE_ROWS):
                rows = slice(lo, lo + _DENSE_ROWS)
                hid = read_ref[it * PEER_N_KEYS + lo:it * PEER_N_KEYS + lo + _DENSE_ROWS, :]
                ge = hid * (1.0 + lax.erf(hid))
                gate = jnp.zeros(hid.shape, BF16)
                for hd in range(PEER_HEADS):
                    sel = lanes(rb_ref, hd, rows) < n_rows[hd]
                    gate = gate + jnp.where(sel, lanes(eb_ref, hd, rows) * f_rows[hd], jnp.zeros((), BF16))
                ws.append((ge * gate.astype(F32)).astype(BF16))
        kw = 2 * PEER_N_KEYS
        per = kw // _DENSE_ROWS
        upd = _dot(vT_ref[:, :kw], jnp.concatenate(ws[:per], axis=0))
        for p in range(1, PEER_STEP_TILES // 2):
            upd = upd + _dot(vT_ref[:, p * kw:(p + 1) * kw], jnp.concatenate(ws[p * per:(p + 1) * per], axis=0))
        acc_ref[...] += upd
        write_ref[...] = hid_new

    @pl.when(i % 2 == 0)
    def _():
        dense_step(hid_b_ref, hid_a_ref)

    @pl.when(i % 2 == 1)
    def _():
        dense_step(hid_a_ref, hid_b_ref)

    @pl.when(i == pl.num_programs(1) - 1)
    def _():
        o_ref[...] = h_ref[...] + acc_ref[...].T


PEER_STEP_TILES = 8
PEER_GROUP = 512


def _peer(h2, g2_row, wqT, sk, u_b, vT_b, tb):
    T = h2.shape[0]
    nk = PEER_N_KEYS
    ne = nk * PEER_STEP_TILES
    n_blocks = nk // PEER_STEP_TILES
    gw = min(tb, PEER_GROUP)
    tile = (PEER_HEADS, tb // gw, nk, gw)
    return pl.pallas_call(
        _peer_kernel,
        grid=(T // tb, n_blocks + 1),
        in_specs=[
            pl.BlockSpec((tb, D_MODEL), lambda t, i: (t, 0)),
            pl.BlockSpec((1, D_MODEL), lambda t, i: (0, 0)),
            pl.BlockSpec((D_MODEL, D_MODEL), lambda t, i: (0, 0)),
            pl.BlockSpec((2, nk, nk // 2), lambda t, i: (0, 0, 0)),
            pl.BlockSpec((ne, D_MODEL), lambda t, i: (jnp.minimum(i, n_blocks - 1), 0)),
            pl.BlockSpec((D_MODEL, ne), lambda t, i: (0, jnp.maximum(i - 1, 0))),
        ],
        out_specs=pl.BlockSpec((tb, D_MODEL), lambda t, i: (t, 0)),
        out_shape=jax.ShapeDtypeStruct((T, D_MODEL), F32),
        scratch_shapes=[
            pltpu.VMEM((D_MODEL, tb), BF16),
            pltpu.VMEM((tb // gw, D_MODEL, gw), BF16),
            pltpu.VMEM(tile, BF16),
            pltpu.VMEM(tile, BF16),
            pltpu.VMEM(tile, F32),
            pltpu.VMEM(tile, F32),
            pltpu.VMEM((D_MODEL, tb), F32),
            pltpu.VMEM((_CAND_ROWS, gw), F32),
            pltpu.VMEM((ne, tb), F32),
            pltpu.VMEM((ne, tb), F32),
        ],
        compiler_params=_cparams(("parallel", "arbitrary")),
        name="peer",
    )(h2, g2_row, wqT, sk, u_b, vT_b)


def _rel_bucket(rel):
    nb = REL_BUCKETS // 2
    ret = jnp.where(rel > 0, nb, 0)
    n = jnp.abs(rel)
    max_exact = nb // 2
    nf = jnp.maximum(n, 1).astype(F32)
    large = max_exact + (jnp.log(nf / max_exact) / math.log(REL_MAX_DIST / max_exact)
                         * (nb - max_exact)).astype(jnp.int32)
    large = jnp.minimum(large, nb - 1)
    return ret + jnp.where(n < max_exact, n, large)


def _bias_tiles(rel_bias, tq):
    kb = ATTN_KB
    n_near = tq // kb + 2
    width = n_near * kb
    period = width + tq
    x = jnp.arange(period, dtype=jnp.int32)
    rel = jnp.where(x < width, x, x - period) - kb
    table = (rel_bias[_rel_bucket(rel)].astype(F32) * LOG2E).T
    skew = jnp.broadcast_to(table[:, None, :], (ATTN_HEADS, tq, period)).reshape(ATTN_HEADS, tq * period)
    skew = skew[:, :tq * (period - 1)].reshape(ATTN_HEADS, tq, period - 1)
    near = skew[:, :, :width].reshape(ATTN_HEADS, tq, n_near, kb).transpose(0, 2, 1, 3)
    far = rel_bias[_rel_bucket(jnp.array([-REL_MAX_DIST, REL_MAX_DIST], jnp.int32))].astype(F32).T * LOG2E
    cfar = jnp.broadcast_to(jnp.pad(far, ((0, 0), (0, SUBLANES - 2)))[:, :, None], (ATTN_HEADS, SUBLANES, LANES))
    return near, cfar


def _pad_rows(a, n):
    return jnp.pad(a, ((0, n - a.shape[0]), (0, 0)))


def _prep(rel_bias, norm1_gain, w_in, conv_w, conv_b, dt_bias_f, dt_bias_b, a_log_f, a_log_b, d_skip,
          ssd_norm_gain, w_ssd_proj, q_norm_gain, k_norm_gain, lambda_q1, lambda_k1, lambda_q2, lambda_k2,
          subln_gain, w_attn_proj, w_out, norm2_gain, peer_w_q, peer_sub_keys, peer_u, peer_v, lam_init):
    o1 = SSD_D_INNER
    o2 = o1 + SSD_CONV_DIM
    o3 = o2 + SSD_N_HEADS
    o4 = o3 + SSD_N_HEADS
    o5 = o4 + 1024
    o6 = o5 + 1024
    o7 = o6 + 1024
    o8 = o7 + D_MODEL
    w = w_in
    zeros = jnp.zeros((D_MODEL, COL_XBC - (COL_GA + D_MODEL)), F32)
    slab = jnp.concatenate([w[:, :o1], w[:, o4:o5], w[:, o5:o6], w[:, o6:o7], w[:, o7:o8], w[:, o8:],
                            zeros, w[:, o1:o2]], axis=1).astype(BF16)
    w_dt = jnp.pad(w[:, o2:o4], ((0, 0), (0, LANES - 2 * SSD_N_HEADS))).astype(BF16)
    row128 = lambda f, b: jnp.pad(jnp.concatenate([f, b]), (0, LANES - 2 * SSD_N_HEADS)).reshape(1, LANES)
    hp = jnp.arange(SSD_D_INNER, dtype=jnp.int32) // SSD_HEAD_DIM
    rows = jnp.arange(LANES, dtype=jnp.int32)[:, None]
    f32 = F32
    lam = (jnp.exp(jnp.sum(lambda_q1.astype(f32) * lambda_k1.astype(f32)))
           - jnp.exp(jnp.sum(lambda_q2.astype(f32) * lambda_k2.astype(f32))) + lam_init)
    return dict(
        gain1=norm1_gain.reshape(1, D_MODEL),
        slab=slab,
        w_dt=w_dt,
        cw=_pad_rows(conv_w.reshape(SSD_CONV, SSD_CONV_DIM), SUBLANES),
        cb=conv_b.reshape(1, SSD_CONV_DIM),
        dtb=row128(dt_bias_f, dt_bias_b),
        alog=row128(a_log_f, a_log_b),
        ex_f=(rows == hp[None, :]).astype(BF16),
        ex_b=(rows == hp[None, :] + SSD_N_HEADS).astype(BF16),
        dsk=jnp.repeat(d_skip, SSD_HEAD_DIM).reshape(1, SSD_D_INNER),
        ng=ssd_norm_gain.reshape(1, SSD_D_INNER),
        wssd=w_ssd_proj.astype(BF16),
        wattn=w_attn_proj.astype(BF16),
        wout=w_out.astype(BF16),
        rel_bias=rel_bias,
        qg=jnp.tile(q_norm_gain, 2).reshape(1, LANES),
        kg=jnp.tile(k_norm_gain, 2).reshape(1, LANES),
        sg=subln_gain.reshape(1, LANES),
        lam=jnp.broadcast_to(lam, (1, LANES)).astype(F32),
        g2=norm2_gain.reshape(1, D_MODEL),
        wqT=peer_w_q.T.astype(BF16),
        sk=peer_sub_keys.astype(BF16),
        u=(peer_u * (2.0 ** -0.5)).astype(BF16),
        vT=(peer_v.T * (2.0 ** -0.5)).astype(BF16),
    )


def _largest_divisor(n, cap):
    t = cap
    while n % t:
        t //= 2
    return t


def _layer(x, p, lam_init):
    B, S, _ = x.shape
    T = B * S
    x2 = x.reshape(T, D_MODEL)
    tm = _largest_divisor(T, 1024)
    proj2, dt2 = _inproj(x2, p["gain1"], p["slab"], p["w_dt"], tm)
    proj3 = proj2.reshape(B, S, PROJ_W)
    dt3 = dt2.reshape(B, S, LANES)
    xbc3 = _conv(proj3, p["cw"], p["cb"])
    yf = _ssd_sweep(xbc3, dt3, p["dtb"], p["alog"], p["ex_f"], rev=False)
    yb = _ssd_sweep(xbc3, dt3, p["dtb"], p["alog"], p["ex_b"], rev=True)
    attn = _attention(proj3, p["rel_bias"], p["qg"], p["kg"], p["sg"], p["lam"], lam_init)
    h2 = _mix(x2, yf.reshape(T, -1), yb.reshape(T, -1), xbc3.reshape(T, -1), proj2, attn.reshape(T, -1),
              p["dsk"], p["ng"], p["wssd"], p["wattn"], p["wout"], _largest_divisor(T, 512))
    y2 = _peer(h2, p["g2"], p["wqT"], p["sk"], p["u"], p["vT"], _largest_divisor(T, 512))
    return y2.reshape(B, S, D_MODEL)


def kernel(x_prompt, x_sample, rel_bias, norm1_gain, w_in, conv_w, conv_b, dt_bias_f, dt_bias_b, a_log_f, a_log_b, d_skip, ssd_norm_gain, w_ssd_proj, q_norm_gain, k_norm_gain, lambda_q1, lambda_k1, lambda_q2, lambda_k2, subln_gain, w_attn_proj, w_out, norm2_gain, peer_w_q, peer_sub_keys, peer_u, peer_v):
    depth = norm1_gain.shape[0]
    lam_inits = [0.8 - 0.6 * math.exp(-0.3 * i) for i in range(depth)]
    preps = [
        _prep(rel_bias, norm1_gain[i], w_in[i], conv_w[i], conv_b[i], dt_bias_f[i], dt_bias_b[i],
              a_log_f[i], a_log_b[i], d_skip[i], ssd_norm_gain[i], w_ssd_proj[i], q_norm_gain[i],
              k_norm_gain[i], lambda_q1[i], lambda_k1[i], lambda_q2[i], lambda_k2[i], subln_gain[i],
              w_attn_proj[i], w_out[i], norm2_gain[i], peer_w_q[i], peer_sub_keys[i], peer_u[i],
              peer_v[i], lam_inits[i])
        for i in range(depth)
    ]
    outs = []
    for x in (x_prompt, x_sample):
        for p, lam_init in zip(preps, lam_inits):
            x = _layer(x, p, lam_init)
        outs.append(x)
    return tuple(outs)
```

```python
import functools
import math

import jax
import jax.numpy as jnp
from jax import lax
from jax.experimental import pallas as pl
from jax.experimental.pallas import tpu as pltpu

F32 = jnp.float32
BF16 = jnp.bfloat16

D_MODEL = 1024
SSD_D_INNER = 1024
SSD_HEAD_DIM = 64
SSD_N_HEADS = 16
SSD_N_GROUPS = 2
SSD_D_STATE = 64
SSD_CONV = 5
SSD_CHUNK = 128
SSD_CONV_DIM = 1280
ATTN_HEADS = 8
ATTN_HEAD_DIM = 64
ATTN_V_DIM = 128
REL_BUCKETS = 32
REL_MAX_DIST = 128
PEER_N_KEYS = 128
PEER_HEADS = 8
PEER_TOPK = 16
EPS = 1e-6

LANES = 128
SUBLANES = 8
VMEM_LIMIT = 56 * 1024 * 1024

COL_Z = 0
COL_Q = 1024
COL_K = 2048
COL_V = 3072
COL_GS = 4096
COL_GA = 5120
COL_XBC = 6400
PROJ_W = 7680
PROJ_TN = 1536

_CAND = [(a, b) for a in range(PEER_TOPK) for b in range(PEER_TOPK) if (a + 1) * (b + 1) <= PEER_TOPK]
_CAND_ROWS = 56


def _dot(a, b):
    return jnp.dot(a, b, preferred_element_type=F32)


def _dot_nt(a, b):
    return lax.dot_general(a, b, (((1,), (1,)), ((), ())), preferred_element_type=F32)


def _split3(v):
    hi = v.astype(BF16)
    r = v - hi.astype(F32)
    mid = r.astype(BF16)
    lo = (r - mid.astype(F32)).astype(BF16)
    return hi, mid, lo


def _exact_dot_rhs(v, m):
    hi, mid, lo = _split3(v)
    return _dot(hi, m) + _dot(mid, m) + _dot(lo, m)


def _expand_dot(v, m):
    hi = v.astype(BF16)
    lo = (v - hi.astype(F32)).astype(BF16)
    return _dot(hi, m) + _dot(lo, m)


def _exact_dot_lhs(m, v):
    hi, mid, lo = _split3(v)
    return _dot(m, hi) + _dot(m, mid) + _dot(m, lo)


def _cparams(sem, flags=None):
    return pltpu.CompilerParams(dimension_semantics=sem, vmem_limit_bytes=VMEM_LIMIT, flags=flags)


def _inproj_kernel(x_ref, g_ref, w_ref, wdt_ref, o_ref, dt_ref, xn_ref):
    j = pl.program_id(1)

    @pl.when(j == 0)
    def _():
        x = x_ref[...]
        ms = jnp.mean(x * x, axis=-1, keepdims=True)
        xn = (x * lax.rsqrt(ms + EPS) * g_ref[...]).astype(BF16)
        xn_ref[...] = xn
        dt_ref[...] = _dot(xn, wdt_ref[...])

    o_ref[...] = _dot(xn_ref[...], w_ref[...]).astype(BF16)


def _inproj(x2, gain, w_slab, w_dt, tm):
    T = x2.shape[0]
    grid = (T // tm, PROJ_W // PROJ_TN)
    return pl.pallas_call(
        _inproj_kernel,
        grid=grid,
        in_specs=[
            pl.BlockSpec((tm, D_MODEL), lambda i, j: (i, 0)),
            pl.BlockSpec((1, D_MODEL), lambda i, j: (0, 0)),
            pl.BlockSpec((D_MODEL, PROJ_TN), lambda i, j: (0, j)),
            pl.BlockSpec((D_MODEL, LANES), lambda i, j: (0, 0)),
        ],
        out_specs=[
            pl.BlockSpec((tm, PROJ_TN), lambda i, j: (i, j)),
            pl.BlockSpec((tm, LANES), lambda i, j: (i, 0)),
        ],
        out_shape=[
            jax.ShapeDtypeStruct((T, PROJ_W), BF16),
            jax.ShapeDtypeStruct((T, LANES), F32),
        ],
        scratch_shapes=[pltpu.VMEM((tm, D_MODEL), BF16)],
        compiler_params=_cparams(("parallel", "arbitrary")),
        name="inproj",
    )(x2, gain, w_slab, w_dt)


def _conv_kernel(xp_ref, xc_ref, xn_ref, cw_ref, cb_ref, o_ref, pad_ref, *, nc):
    c = pl.program_id(1)
    L = SSD_CHUNK
    H = SUBLANES
    xp = jnp.where(c > 0, xp_ref[...].astype(F32), 0.0)
    xn = jnp.where(c < nc - 1, xn_ref[...].astype(F32), 0.0)
    pad_ref[0:H, :] = xp
    pad_ref[H:H + L, :] = xc_ref[...].astype(F32)
    pad_ref[H + L:2 * H + L, :] = xn
    half = (SSD_CONV - 1) // 2
    acc = jnp.broadcast_to(cb_ref[...], (L, SSD_CONV_DIM))
    for k in range(SSD_CONV):
        off = H - half + k
        acc = acc + cw_ref[k:k + 1, :] * pad_ref[off:off + L, :]
    o_ref[...] = (acc * jax.nn.sigmoid(acc)).astype(BF16)


def _conv(proj3, cw, cb):
    B, S, _ = proj3.shape
    L = SSD_CHUNK
    nc = S // L
    rb = L // SUBLANES
    cblk = COL_XBC // SSD_CONV_DIM
    return pl.pallas_call(
        functools.partial(_conv_kernel, nc=nc),
        grid=(B, nc),
        in_specs=[
            pl.BlockSpec((None, SUBLANES, SSD_CONV_DIM), lambda b, c: (b, jnp.maximum(c * rb - 1, 0), cblk)),
            pl.BlockSpec((None, L, SSD_CONV_DIM), lambda b, c: (b, c, cblk)),
            pl.BlockSpec((None, SUBLANES, SSD_CONV_DIM),
                         lambda b, c: (b, jnp.minimum((c + 1) * rb, S // SUBLANES - 1), cblk)),
            pl.BlockSpec((SUBLANES, SSD_CONV_DIM), lambda b, c: (0, 0)),
            pl.BlockSpec((1, SSD_CONV_DIM), lambda b, c: (0, 0)),
        ],
        out_specs=pl.BlockSpec((None, L, SSD_CONV_DIM), lambda b, c: (b, c, 0)),
        out_shape=jax.ShapeDtypeStruct((B, S, SSD_CONV_DIM), BF16),
        scratch_shapes=[pltpu.VMEM((L + 2 * SUBLANES, SSD_CONV_DIM), F32)],
        compiler_params=_cparams(("parallel", "parallel")),
        name="ssd_conv",
    )(proj3, proj3, proj3, cw, cb)


def _ssd_sweep_kernel(xbc_ref, dt_ref, dtb_ref, alog_ref, ex_ref, y_ref, state_ref, *, rev, hoff):
    c = pl.program_id(1)
    L = SSD_CHUNK

    @pl.when(c == 0)
    def _():
        state_ref[...] = jnp.zeros_like(state_ref)

    xbc = xbc_ref[...]
    xs = xbc[:, :SSD_D_INNER].astype(F32)
    Bm = xbc[:, SSD_D_INNER:SSD_D_INNER + LANES]
    Cm = xbc[:, SSD_D_INNER + LANES:SSD_D_INNER + 2 * LANES]
    ex = ex_ref[...]

    dt = jax.nn.softplus(dt_ref[...] + dtb_ref[...])
    a = dt * (-jnp.exp(alog_ref[...]))
    row = lax.broadcasted_iota(jnp.int32, (L, L), 0)
    col = lax.broadcasted_iota(jnp.int32, (L, L), 1)
    tri = (col >= row) if rev else (col <= row)
    cs = _exact_dot_lhs(tri.astype(BF16), a)
    csT = cs.T
    end = 0 if rev else L - 1
    cs_end = cs[end:end + 1, :]

    dtT = dt.T
    xs_b = xbc[:, :SSD_D_INNER]
    xw = (xs * _expand_dot(dt * jnp.exp(cs_end - cs), ex)).astype(BF16)

    lane = lax.broadcasted_iota(jnp.int32, (1, LANES), 1)
    g0 = lane < SSD_D_STATE
    zero_b = jnp.zeros((), BF16)
    cb0 = _dot_nt(jnp.where(g0, Cm, zero_b), Bm)
    cb1 = _dot_nt(jnp.where(g0, zero_b, Cm), Bm)

    ys = []
    for pair in range(SSD_N_HEADS // 2):
        ms = []
        for hh in (2 * pair, 2 * pair + 1):
            k = hoff + hh
            seg = cs[:, k:k + 1] - csT[k:k + 1, :]
            dec = jnp.exp(jnp.where(tri, seg, -jnp.inf))
            cbg = cb0 if hh < SSD_N_HEADS // 2 else cb1
            ms.append((cbg * dec * dtT[k:k + 1, :]).astype(BF16))
        lhs = jnp.concatenate(ms, axis=1)
        xp = xs_b[:, LANES * pair:LANES * (pair + 1)]
        rhs = jnp.concatenate([jnp.where(g0, xp, zero_b), jnp.where(g0, zero_b, xp)], axis=0)
        ys.append(_dot(lhs, rhs))
    y_diag = jnp.concatenate(ys, axis=1)

    st = state_ref[...]
    y_off = _dot(Cm, st.astype(BF16)) * _expand_dot(jnp.exp(cs), ex)
    y_ref[...] = y_diag + y_off

    dec_c = _expand_dot(jnp.broadcast_to(jnp.exp(cs_end), (SUBLANES, LANES)), ex)[0:1, :]
    BmT = Bm.astype(F32).T.astype(BF16)
    upd = _dot(BmT, xw)
    r2 = lax.broadcasted_iota(jnp.int32, (LANES, SSD_D_INNER), 0) // SSD_D_STATE
    c2 = lax.broadcasted_iota(jnp.int32, (LANES, SSD_D_INNER), 1) // (SSD_D_INNER // SSD_N_GROUPS)
    state_ref[...] = st * dec_c + jnp.where(r2 == c2, upd, 0.0)


def _ssd_sweep(xbc3, dt3, dtb_row, alog_row, ex, rev):
    B, S, _ = xbc3.shape
    L = SSD_CHUNK
    nc = S // L
    hoff = SSD_N_HEADS if rev else 0
    if rev:
        cmap = lambda b, c: (b, nc - 1 - c, 0)
    else:
        cmap = lambda b, c: (b, c, 0)
    return pl.pallas_call(
        functools.partial(_ssd_sweep_kernel, rev=rev, hoff=hoff),
        grid=(B, nc),
        in_specs=[
            pl.BlockSpec((None, L, SSD_CONV_DIM), cmap),
            pl.BlockSpec((None, L, LANES), cmap),
            pl.BlockSpec((1, LANES), lambda b, c: (0, 0)),
            pl.BlockSpec((1, LANES), lambda b, c: (0, 0)),
            pl.BlockSpec((LANES, SSD_D_INNER), lambda b, c: (0, 0)),
        ],
        out_specs=pl.BlockSpec((None, L, SSD_D_INNER), cmap),
        out_shape=jax.ShapeDtypeStruct((B, S, SSD_D_INNER), F32),
        scratch_shapes=[pltpu.VMEM((LANES, SSD_D_INNER), F32)],
        compiler_params=_cparams(("parallel", "arbitrary")),
        name="ssd_sweep_rev" if rev else "ssd_sweep_fwd",
    )(xbc3, dt3, dtb_row, alog_row, ex)


ATTN_TQ_MAX = 512
ATTN_SCORE_BYTES = 16 * 1024 * 1024
ATTN_KB = LANES
ATTN_PV_TILES = 4
ATTN_QK_UNROLL_MAX = 12
LOG2E = math.log2(math.e)


def _attn_kernel(q_ref, k_ref, v_ref, bias_ref, cfar_ref, qg_ref, kg_ref, sg_ref, lam_ref, o_ref,
                 kn_ref, s1_ref, s2_ref, mx1_ref, mx2_ref, *, nk, scale, out_scale, qk_unroll):
    qi = pl.program_id(2)
    tq = q_ref.shape[0]
    kb = ATTN_KB
    n_near = tq // kb + 2
    r = lax.broadcasted_iota(jnp.int32, (LANES, LANES), 0) // ATTN_HEAD_DIM
    c = lax.broadcasted_iota(jnp.int32, (LANES, LANES), 1) // ATTN_HEAD_DIM
    ones2 = (r == c).astype(BF16)
    lane = lax.broadcasted_iota(jnp.int32, (1, LANES), 1)
    first = lane < ATTN_HEAD_DIM
    zero_b = jnp.zeros((), BF16)

    def halfnorm(x, g):
        ms = _exact_dot_rhs(x * x, ones2) * (1.0 / ATTN_HEAD_DIM)
        return x * lax.rsqrt(ms + EPS) * g

    @pl.when(qi == 0)
    def _():
        cl = cfar_ref[0:1, :]
        cr = cfar_ref[1:2, :]
        cl_hi = cl.astype(BF16).astype(F32)
        cr_hi = cr.astype(BF16).astype(F32)
        rowi = lax.broadcasted_iota(jnp.int32, (LANES, LANES), 0)
        top = rowi < ATTN_HEAD_DIM
        cpart = jnp.where(rowi == 0, cl_hi, jnp.where(rowi == 1, cl - cl_hi,
                          jnp.where(rowi == 2, cr_hi, jnp.where(rowi == 3, cr - cr_hi, 0.0)))).astype(BF16)
        cpart = jnp.concatenate([cpart, cpart], axis=1)

        def prep(t, carry):
            src = pl.multiple_of(t * kb, kb)
            knT = halfnorm(k_ref[pl.ds(src, kb), :].astype(F32), kg_ref[...]).T
            kpart = jnp.concatenate([jnp.where(top, knT, 0.0), jnp.where(top, 0.0, knT)], axis=1).astype(BF16)
            kn_ref[t] = jnp.concatenate([kpart, cpart], axis=0)
            return carry

        lax.fori_loop(0, nk, prep, 0, unroll=4)

    qn = (halfnorm(q_ref[...].astype(F32), qg_ref[...]) * (scale * LOG2E)).astype(BF16)
    sw_l = jnp.broadcast_to(jnp.where(lane < 2, 1.0, 0.0), (tq, LANES)).astype(BF16)
    sw_r = jnp.broadcast_to(jnp.where((lane >= 2) & (lane < 4), 1.0, 0.0), (tq, LANES)).astype(BF16)
    q_left = jnp.concatenate([qn, sw_l], axis=1)
    q_right = jnp.concatenate([qn, sw_r], axis=1)
    q_near = jnp.concatenate([qn, jnp.zeros((tq, LANES), BF16)], axis=1)

    def score_tile(t, qz, bias):
        s12 = _dot(qz, kn_ref[t])
        s1 = s12[:, :kb]
        s2 = s12[:, kb:]
        if bias is not None:
            s1 = s1 + bias
            s2 = s2 + bias
        s1_ref[t] = s1
        s2_ref[t] = s2
        return s1, s2

    near_lo = qi * (tq // kb) - 1
    n_left = jnp.maximum(near_lo, 0)
    near_hi = jnp.minimum(near_lo + n_near, nk)
    n_far = n_left + (nk - near_hi)

    def far_body(it, carry):
        m1, m2 = carry
        for u in range(qk_unroll):
            f = jnp.minimum(it * qk_unroll + u, n_far - 1)
            t = jnp.where(f < n_left, f, f - n_left + near_hi)
            s1, s2 = score_tile(t, jnp.where(f < n_left, q_left, q_right), None)
            m1 = jnp.maximum(m1, s1)
            m2 = jnp.maximum(m2, s2)
        return m1, m2

    minf = jnp.full((tq, LANES), -jnp.inf, F32)
    far_m1, far_m2 = lax.fori_loop(0, (n_far + qk_unroll - 1) // qk_unroll, far_body, (minf, minf))
    interior = (near_lo >= 0) & (near_lo + n_near <= nk)

    @pl.when(interior)
    def _():
        m1, m2 = far_m1, far_m2
        for j in range(n_near):
            s1, s2 = score_tile(near_lo + j, q_near, bias_ref[j])
            m1 = jnp.maximum(m1, s1)
            m2 = jnp.maximum(m2, s2)
        mx1_ref[...] = m1
        mx2_ref[...] = m2

    @pl.when(jnp.logical_not(interior))
    def _():
        mx1_ref[...] = far_m1
        mx2_ref[...] = far_m2
        for j in range(n_near):
            t = near_lo + j

            @pl.when((t >= 0) & (t < nk))
            def _():
                s1, s2 = score_tile(t, q_near, bias_ref[j])
                mx1_ref[...] = jnp.maximum(mx1_ref[...], s1)
                mx2_ref[...] = jnp.maximum(mx2_ref[...], s2)

    m1 = jnp.max(mx1_ref[...], axis=1, keepdims=True)
    m2 = jnp.max(mx2_ref[...], axis=1, keepdims=True)

    def pv_body(ch, carry):
        l1, l2, a1, a2 = carry
        p1s, p2s = [], []
        for u in range(ATTN_PV_TILES):
            t = ch * ATTN_PV_TILES + u
            p1 = jnp.exp2(s1_ref[t] - m1)
            p2 = jnp.exp2(s2_ref[t] - m2)
            l1 = l1 + p1
            l2 = l2 + p2
            p1s.append(p1.astype(BF16))
            p2s.append(p2.astype(BF16))
        vch = v_ref[pl.ds(pl.multiple_of(ch * ATTN_PV_TILES * kb, ATTN_PV_TILES * kb), ATTN_PV_TILES * kb), :]
        a1 = a1 + _dot(jnp.concatenate(p1s, axis=1), vch)
        a2 = a2 + _dot(jnp.concatenate(p2s, axis=1), vch)
        return l1, l2, a1, a2

    z = jnp.zeros((tq, LANES), F32)
    l1, l2, a1, a2 = lax.fori_loop(0, nk // ATTN_PV_TILES, pv_body, (z, z, z, z), unroll=4)
    l1 = jnp.sum(l1, axis=1, keepdims=True)
    l2 = jnp.sum(l2, axis=1, keepdims=True)
    o = a1 / l1 - lam_ref[...] * (a2 / l2)
    ms = jnp.mean(o * o, axis=-1, keepdims=True)
    o_ref[...] = (o * lax.rsqrt(ms + EPS) * sg_ref[...] * out_scale).astype(BF16)


def _attn_tq(S):
    tq = ATTN_TQ_MAX
    while 2 * 4 * S * tq > ATTN_SCORE_BYTES or S % tq:
        tq //= 2
    return tq


def _attention(proj3, rel_bias, qg, kg, sg, lam_row, lam_init):
    B, S, _ = proj3.shape
    tq = _attn_tq(S)
    bias_near, cfar = _bias_tiles(rel_bias, tq)
    nq = S // tq
    nk = S // ATTN_KB
    n_near = tq // ATTN_KB + 2
    n_far = max(nk - n_near, 1)
    qk_unroll = max(d for d in range(1, ATTN_QK_UNROLL_MAX + 1) if n_far % d == 0)
    return pl.pallas_call(
        functools.partial(_attn_kernel, nk=nk, scale=ATTN_HEAD_DIM ** -0.5, out_scale=1.0 - lam_init,
                          qk_unroll=qk_unroll),
        grid=(B, ATTN_HEADS, nq),
        in_specs=[
            pl.BlockSpec((None, tq, LANES), lambda b, h, q: (b, q, COL_Q // LANES + h)),
            pl.BlockSpec((None, S, LANES), lambda b, h, q: (b, 0, COL_K // LANES + h)),
            pl.BlockSpec((None, S, LANES), lambda b, h, q: (b, 0, COL_V // LANES + h)),
            pl.BlockSpec((None, n_near, tq, LANES), lambda b, h, q: (h, 0, 0, 0)),
            pl.BlockSpec((None, SUBLANES, LANES), lambda b, h, q: (h, 0, 0)),
            pl.BlockSpec((1, LANES), lambda b, h, q: (0, 0)),
            pl.BlockSpec((1, LANES), lambda b, h, q: (0, 0)),
            pl.BlockSpec((1, LANES), lambda b, h, q: (0, 0)),
            pl.BlockSpec((1, LANES), lambda b, h, q: (0, 0)),
        ],
        out_specs=pl.BlockSpec((None, tq, LANES), lambda b, h, q: (b, q, h)),
        out_shape=jax.ShapeDtypeStruct((B, S, ATTN_HEADS * ATTN_V_DIM), BF16),
        scratch_shapes=[
            pltpu.VMEM((nk, 2 * LANES, 2 * LANES), BF16),
            pltpu.VMEM((nk, tq, LANES), F32),
            pltpu.VMEM((nk, tq, LANES), F32),
            pltpu.VMEM((tq, LANES), F32),
            pltpu.VMEM((tq, LANES), F32),
        ],
        compiler_params=_cparams(("parallel", "parallel", "arbitrary")),
        name="diff_attn",
    )(proj3, proj3, proj3, bias_near, cfar, qg, kg, sg, lam_row)


def _mix_kernel(x_ref, yf_ref, yb_ref, xbc_ref, z_ref, at_ref, gs_ref, ga_ref, dsk_ref, ng_ref,
                wssd_ref, wattn_ref, wout_ref, h_ref):
    xs = xbc_ref[...][:, :SSD_D_INNER].astype(F32)
    y = yf_ref[...] + yb_ref[...] + xs * dsk_ref[...]
    z = z_ref[...].astype(F32)
    y = y * (z * jax.nn.sigmoid(z))
    gw = SSD_D_INNER // SSD_N_GROUPS
    parts = []
    for g in range(SSD_N_GROUPS):
        yg = y[:, g * gw:(g + 1) * gw]
        parts.append(yg * lax.rsqrt(jnp.mean(yg * yg, axis=-1, keepdims=True) + EPS))
    yn = (jnp.concatenate(parts, axis=1) * ng_ref[...]).astype(BF16)
    y_ssd = _dot(yn, wssd_ref[...])
    y_attn = _dot(at_ref[...], wattn_ref[...])
    mixed = (jax.nn.sigmoid(gs_ref[...].astype(F32)) * y_ssd
             + jax.nn.sigmoid(ga_ref[...].astype(F32)) * y_attn)
    h_ref[...] = x_ref[...] + _dot(mixed.astype(BF16), wout_ref[...])


def _mix(x2, yf2, yb2, xbc2, proj2, attn2, dsk_row, ng_row, wssd, wattn, wout, tm):
    T = x2.shape[0]
    row = lambda i: (i, 0)
    const = lambda i: (0, 0)
    return pl.pallas_call(
        _mix_kernel,
        grid=(T // tm,),
        in_specs=[
            pl.BlockSpec((tm, D_MODEL), row),
            pl.BlockSpec((tm, SSD_D_INNER), row),
            pl.BlockSpec((tm, SSD_D_INNER), row),
            pl.BlockSpec((tm, SSD_CONV_DIM), row),
            pl.BlockSpec((tm, D_MODEL), lambda i: (i, COL_Z // D_MODEL)),
            pl.BlockSpec((tm, D_MODEL), row),
            pl.BlockSpec((tm, D_MODEL), lambda i: (i, COL_GS // D_MODEL)),
            pl.BlockSpec((tm, D_MODEL), lambda i: (i, COL_GA // D_MODEL)),
            pl.BlockSpec((1, D_MODEL), const),
            pl.BlockSpec((1, D_MODEL), const),
            pl.BlockSpec((D_MODEL, D_MODEL), const),
            pl.BlockSpec((D_MODEL, D_MODEL), const),
            pl.BlockSpec((D_MODEL, D_MODEL), const),
        ],
        out_specs=pl.BlockSpec((tm, D_MODEL), row),
        out_shape=jax.ShapeDtypeStruct((T, D_MODEL), F32),
        compiler_params=_cparams(("parallel",)),
        name="mix",
    )(x2, yf2, yb2, xbc2, proj2, attn2, proj2, proj2, dsk_row, ng_row, wssd, wattn, wout)


def _extract_topk(work, n):
    iota = lax.broadcasted_iota(jnp.int32, work.shape, 0)
    big = jnp.int32(work.shape[0])
    rank = jnp.full(work.shape, n, jnp.int32)
    vals = []
    for r in range(n):
        m = jnp.max(work, axis=0, keepdims=True)
        sel = iota == jnp.min(jnp.where(work == m, iota, big), axis=0, keepdims=True)
        rank = jnp.where(sel, r, rank)
        work = jnp.where(sel, -jnp.inf, work)
        vals.append(m)
    return rank, vals


def _topk_values(s, n):
    m = jnp.max(s, axis=0, keepdims=True)
    vals = [m]
    for _ in range(1, n):
        m = jnp.max(jnp.where(s < m, s, -jnp.inf), axis=0, keepdims=True)
        vals.append(m)
    return vals


_ROUTE_ROWS = 16
_DENSE_ROWS = 32


def _count(mask):
    return jnp.sum(jnp.where(mask, 1.0, 0.0), axis=0, keepdims=True)


def _fill_candidates(cand_ref, top0, top1):
    for r, (a, b) in enumerate(_CAND):
        cand_ref[r:r + 1, :] = top0[a] + top1[b]
    return cand_ref[...]


def _route_exact(s0, s1, cand_ref):
    K = PEER_TOPK
    rank0, top0 = _extract_topk(s0, K)
    rank1, top1 = _extract_topk(s1, K)
    cand = _fill_candidates(cand_ref, top0, top1)
    crank, cvals = _extract_topk(cand, K)
    csel = crank < K
    zsum = jnp.sum(jnp.where(csel, jnp.exp(cand - cvals[0]), 0.0), axis=0, keepdims=True)
    crow = lax.broadcasted_iota(jnp.int32, cand.shape, 0)
    n_i = jnp.zeros(s0.shape, F32)
    r0 = 0
    for a in range(K):
        r1 = r0 + K // (a + 1)
        n_a = _count(csel & (crow >= r0) & (crow < r1))
        n_i = n_i + jnp.where(rank0 == a, n_a, 0.0)
        r0 = r1
    f_i = jnp.where(rank0 < K, jnp.exp(s0 - top0[0]), 0.0)
    e_b = jnp.where(rank1 < K, jnp.exp(s1 - top1[0]) / zsum, 0.0)
    return n_i, f_i, rank1.astype(F32), e_b


def _route_fast(s0, s1, cand_ref):
    K = PEER_TOPK
    top0 = _topk_values(s0, K)
    top1 = _topk_values(s1, K)
    cand = _fill_candidates(cand_ref, top0, top1)
    ctop = _topk_values(cand, K)
    tau = ctop[K - 1]
    in0 = s0 >= top0[K - 1]
    in1 = s1 >= top1[K - 1]
    n_parts, rank_parts = [], []
    for lo in range(0, PEER_N_KEYS, _ROUTE_ROWS):
        rows0 = s0[lo:lo + _ROUTE_ROWS]
        rows1 = s1[lo:lo + _ROUTE_ROWS]
        n_p = jnp.zeros(rows0.shape, F32)
        r_p = jnp.zeros(rows1.shape, F32)
        for r in range(K):
            n_p = n_p + jnp.where(rows0 + top1[r] >= tau, 1.0, 0.0)
            r_p = r_p + jnp.where(rows1 < top1[r], 1.0, 0.0)
        n_parts.append(n_p)
        rank_parts.append(r_p)
    n_i = jnp.where(in0, jnp.concatenate(n_parts, axis=0), 0.0)
    rank1 = jnp.concatenate(rank_parts, axis=0)
    zsum = jnp.sum(jnp.where(cand >= tau, jnp.exp(cand - ctop[0]), 0.0), axis=0, keepdims=True)
    f_i = jnp.where(in0, jnp.exp(s0 - top0[0]), 0.0)
    e_b = jnp.where(in1, jnp.exp(s1 - top1[0]) / zsum, 0.0)
    bad = (jnp.abs(_count(in0) - K) + jnp.abs(_count(in1) - K)
           + jnp.abs(jnp.sum(n_i, axis=0, keepdims=True) - K))
    return (n_i, f_i, rank1, e_b), jnp.max(bad) == 0.0


def _peer_kernel(h_ref, g2_ref, wqT_ref, sk_ref, u_ref, vT_ref, o_ref,
                 hnT_ref, qT_ref, rb_ref, eb_ref, ni_ref, fi_ref, acc_ref, cand_ref, hid_a_ref, hid_b_ref):
    i = pl.program_id(1)
    K = PEER_TOPK
    tb = h_ref.shape[0]
    gw = qT_ref.shape[2]
    n_groups = tb // gw
    half = PEER_N_KEYS // 2

    @pl.when(i == 0)
    def _router():
        h = h_ref[...]
        hn = h * lax.rsqrt(jnp.mean(h * h, axis=-1, keepdims=True) + EPS) * g2_ref[...]
        hnT = hn.T.astype(BF16)
        hnT_ref[...] = hnT
        qT = _dot(wqT_ref[...], hnT).astype(BF16)
        for g in range(n_groups):
            qT_ref[g] = qT[:, g * gw:(g + 1) * gw]
        acc_ref[...] = jnp.zeros_like(acc_ref)
        hid_b_ref[...] = jnp.zeros_like(hid_b_ref)
        cand_ref[...] = jnp.full(cand_ref.shape, -jnp.inf, F32)

        def route(x, carry):
            hd = x // n_groups
            g = x % n_groups
            scores = []
            for m in range(2):
                off = pl.multiple_of(hd * PEER_N_KEYS + m * half, half)
                scores.append(_dot(sk_ref[m], qT_ref[g, pl.ds(off, half), :]))
            fast, ok = _route_fast(scores[0], scores[1], cand_ref)
            n_i, f_i, rank1, e_b = lax.cond(ok, lambda: fast, lambda: _route_exact(scores[0], scores[1], cand_ref))
            ni_ref[hd, g] = n_i
            fi_ref[hd, g] = f_i
            rb_ref[hd, g] = rank1.astype(BF16)
            eb_ref[hd, g] = e_b.astype(BF16)
            return carry

        lax.fori_loop(0, PEER_HEADS * n_groups, route, 0)

    def lanes(ref, hd, rows):
        return jnp.concatenate([ref[hd, g, rows, :] for g in range(n_groups)], axis=1)

    def dense_step(read_ref, write_ref):
        hid_new = _dot(u_ref[...], hnT_ref[...])
        prev = jnp.maximum(i - 1, 0)
        ws = []
        for it in range(PEER_STEP_TILES):
            key0 = prev * PEER_STEP_TILES + it
            n_rows = [lanes(ni_ref, hd, pl.ds(key0, 1)).astype(BF16) for hd in range(PEER_HEADS)]
            f_rows = [lanes(fi_ref, hd, pl.ds(key0, 1)).astype(BF16) for hd in range(PEER_HEADS)]
            for lo in range(0, PEER_N_KEYS, _DENSE_ROWS):
                rows = slice(lo, lo + _DENSE_ROWS)
                hid = read_ref[it * PEER_N_KEYS + lo:it * PEER_N_KEYS + lo + _DENSE_ROWS, :]
                hb = hid.astype(BF16)
                ge = hb * (lax.erf(hb) + jnp.ones((), BF16))
                gate = jnp.zeros(hid.shape, BF16)
                for hd in range(PEER_HEADS):
                    sel = lanes(rb_ref, hd, rows) < n_rows[hd]
                    gate = gate + jnp.where(sel, lanes(eb_ref, hd, rows) * f_rows[hd], jnp.zeros((), BF16))
                ws.append(ge * gate)
        kw = 2 * PEER_N_KEYS
        per = kw // _DENSE_ROWS
        upd = _dot(vT_ref[:, :kw], jnp.concatenate(ws[:per], axis=0))
        for p in range(1, PEER_STEP_TILES // 2):
            upd = upd + _dot(vT_ref[:, p * kw:(p + 1) * kw], jnp.concatenate(ws[p * per:(p + 1) * per], axis=0))
        acc_ref[...] += upd
        write_ref[...] = hid_new

    @pl.when(i % 2 == 0)
    def _():
        dense_step(hid_b_ref, hid_a_ref)

    @pl.when(i % 2 == 1)
    def _():
        dense_step(hid_a_ref, hid_b_ref)

    @pl.when(i == pl.num_programs(1) - 1)
    def _():
        o_ref[...] = h_ref[...] + acc_ref[...].T


PEER_STEP_TILES = 8
PEER_GROUP = 512


def _peer(h2, g2_row, wqT, sk, u_b, vT_b, tb):
    T = h2.shape[0]
    nk = PEER_N_KEYS
    ne = nk * PEER_STEP_TILES
    n_blocks = nk // PEER_STEP_TILES
    gw = min(tb, PEER_GROUP)
    tile = (PEER_HEADS, tb // gw, nk, gw)
    return pl.pallas_call(
        _peer_kernel,
        grid=(T // tb, n_blocks + 1),
        in_specs=[
            pl.BlockSpec((tb, D_MODEL), lambda t, i: (t, 0)),
            pl.BlockSpec((1, D_MODEL), lambda t, i: (0, 0)),
            pl.BlockSpec((D_MODEL, D_MODEL), lambda t, i: (0, 0)),
            pl.BlockSpec((2, nk, nk // 2), lambda t, i: (0, 0, 0)),
            pl.BlockSpec((ne, D_MODEL), lambda t, i: (jnp.minimum(i, n_blocks - 1), 0)),
            pl.BlockSpec((D_MODEL, ne), lambda t, i: (0, jnp.maximum(i - 1, 0))),
        ],
        out_specs=pl.BlockSpec((tb, D_MODEL), lambda t, i: (t, 0)),
        out_shape=jax.ShapeDtypeStruct((T, D_MODEL), F32),
        scratch_shapes=[
            pltpu.VMEM((D_MODEL, tb), BF16),
            pltpu.VMEM((tb // gw, D_MODEL, gw), BF16),
            pltpu.VMEM(tile, BF16),
            pltpu.VMEM(tile, BF16),
            pltpu.VMEM(tile, F32),
            pltpu.VMEM(tile, F32),
            pltpu.VMEM((D_MODEL, tb), F32),
            pltpu.VMEM((_CAND_ROWS, gw), F32),
            pltpu.VMEM((ne, tb), F32),
            pltpu.VMEM((ne, tb), F32),
        ],
        compiler_params=_cparams(("parallel", "arbitrary")),
        name="peer",
    )(h2, g2_row, wqT, sk, u_b, vT_b)


def _rel_bucket(rel):
    nb = REL_BUCKETS // 2
    ret = jnp.where(rel > 0, nb, 0)
    n = jnp.abs(rel)
    max_exact = nb // 2
    nf = jnp.maximum(n, 1).astype(F32)
    large = max_exact + (jnp.log(nf / max_exact) / math.log(REL_MAX_DIST / max_exact)
                         * (nb - max_exact)).astype(jnp.int32)
    large = jnp.minimum(large, nb - 1)
    return ret + jnp.where(n < max_exact, n, large)


def _bias_tiles(rel_bias, tq):
    kb = ATTN_KB
    n_near = tq // kb + 2
    width = n_near * kb
    rel = jnp.arange(width + tq - 1, dtype=jnp.int32) - (tq - 1) - kb
    table = (rel_bias[_rel_bucket(rel)].astype(F32) * LOG2E).T
    rows = jnp.stack([table[:, tq - 1 - r:tq - 1 - r + width] for r in range(tq)], axis=1)
    near = rows.reshape(ATTN_HEADS, tq, n_near, kb).transpose(0, 2, 1, 3)
    far = rel_bias[_rel_bucket(jnp.array([-REL_MAX_DIST, REL_MAX_DIST], jnp.int32))].astype(F32).T * LOG2E
    cfar = jnp.broadcast_to(jnp.pad(far, ((0, 0), (0, SUBLANES - 2)))[:, :, None], (ATTN_HEADS, SUBLANES, LANES))
    return near, cfar


def _pad_rows(a, n):
    return jnp.pad(a, ((0, n - a.shape[0]), (0, 0)))


def _prep(rel_bias, norm1_gain, w_in, conv_w, conv_b, dt_bias_f, dt_bias_b, a_log_f, a_log_b, d_skip,
          ssd_norm_gain, w_ssd_proj, q_norm_gain, k_norm_gain, lambda_q1, lambda_k1, lambda_q2, lambda_k2,
          subln_gain, w_attn_proj, w_out, norm2_gain, peer_w_q, peer_sub_keys, peer_u, peer_v, lam_init):
    o1 = SSD_D_INNER
    o2 = o1 + SSD_CONV_DIM
    o3 = o2 + SSD_N_HEADS
    o4 = o3 + SSD_N_HEADS
    o5 = o4 + 1024
    o6 = o5 + 1024
    o7 = o6 + 1024
    o8 = o7 + D_MODEL
    w = w_in
    zeros = jnp.zeros((D_MODEL, COL_XBC - (COL_GA + D_MODEL)), F32)
    slab = jnp.concatenate([w[:, :o1], w[:, o4:o5], w[:, o5:o6], w[:, o6:o7], w[:, o7:o8], w[:, o8:],
                            zeros, w[:, o1:o2]], axis=1).astype(BF16)
    w_dt = jnp.pad(w[:, o2:o4], ((0, 0), (0, LANES - 2 * SSD_N_HEADS))).astype(BF16)
    row128 = lambda f, b: jnp.pad(jnp.concatenate([f, b]), (0, LANES - 2 * SSD_N_HEADS)).reshape(1, LANES)
    hp = jnp.arange(SSD_D_INNER, dtype=jnp.int32) // SSD_HEAD_DIM
    rows = jnp.arange(LANES, dtype=jnp.int32)[:, None]
    f32 = F32
    lam = (jnp.exp(jnp.sum(lambda_q1.astype(f32) * lambda_k1.astype(f32)))
           - jnp.exp(jnp.sum(lambda_q2.astype(f32) * lambda_k2.astype(f32))) + lam_init)
    return dict(
        gain1=norm1_gain.reshape(1, D_MODEL),
        slab=slab,
        w_dt=w_dt,
        cw=_pad_rows(conv_w.reshape(SSD_CONV, SSD_CONV_DIM), SUBLANES),
        cb=conv_b.reshape(1, SSD_CONV_DIM),
        dtb=row128(dt_bias_f, dt_bias_b),
        alog=row128(a_log_f, a_log_b),
        ex_f=(rows == hp[None, :]).astype(BF16),
        ex_b=(rows == hp[None, :] + SSD_N_HEADS).astype(BF16),
        dsk=jnp.repeat(d_skip, SSD_HEAD_DIM).reshape(1, SSD_D_INNER),
        ng=ssd_norm_gain.reshape(1, SSD_D_INNER),
        wssd=w_ssd_proj.astype(BF16),
        wattn=w_attn_proj.astype(BF16),
        wout=w_out.astype(BF16),
        rel_bias=rel_bias,
        qg=jnp.tile(q_norm_gain, 2).reshape(1, LANES),
        kg=jnp.tile(k_norm_gain, 2).reshape(1, LANES),
        sg=subln_gain.reshape(1, LANES),
        lam=jnp.broadcast_to(lam, (1, LANES)).astype(F32),
        g2=norm2_gain.reshape(1, D_MODEL),
        wqT=peer_w_q.T.astype(BF16),
        sk=peer_sub_keys.astype(BF16),
        u=(peer_u * (2.0 ** -0.5)).astype(BF16),
        vT=(peer_v.T * (2.0 ** -0.5)).astype(BF16),
    )


def _largest_divisor(n, cap):
    t = cap
    while n % t:
        t //= 2
    return t


def _layer(x, p, lam_init):
    B, S, _ = x.shape
    T = B * S
    x2 = x.reshape(T, D_MODEL)
    tm = _largest_divisor(T, 1024)
    proj2, dt2 = _inproj(x2, p["gain1"], p["slab"], p["w_dt"], tm)
    proj3 = proj2.reshape(B, S, PROJ_W)
    dt3 = dt2.reshape(B, S, LANES)
    xbc3 = _conv(proj3, p["cw"], p["cb"])
    yf = _ssd_sweep(xbc3, dt3, p["dtb"], p["alog"], p["ex_f"], rev=False)
    yb = _ssd_sweep(xbc3, dt3, p["dtb"], p["alog"], p["ex_b"], rev=True)
    attn = _attention(proj3, p["rel_bias"], p["qg"], p["kg"], p["sg"], p["lam"], lam_init)
    h2 = _mix(x2, yf.reshape(T, -1), yb.reshape(T, -1), xbc3.reshape(T, -1), proj2, attn.reshape(T, -1),
              p["dsk"], p["ng"], p["wssd"], p["wattn"], p["wout"], _largest_divisor(T, 512))
    y2 = _peer(h2, p["g2"], p["wqT"], p["sk"], p["u"], p["vT"], _largest_divisor(T, 512))
    return y2.reshape(B, S, D_MODEL)


def kernel(x_prompt, x_sample, rel_bias, norm1_gain, w_in, conv_w, conv_b, dt_bias_f, dt_bias_b, a_log_f, a_log_b, d_skip, ssd_norm_gain, w_ssd_proj, q_norm_gain, k_norm_gain, lambda_q1, lambda_k1, lambda_q2, lambda_k2, subln_gain, w_attn_proj, w_out, norm2_gain, peer_w_q, peer_sub_keys, peer_u, peer_v):
    depth = norm1_gain.shape[0]
    lam_inits = [0.8 - 0.6 * math.exp(-0.3 * i) for i in range(depth)]
    preps = [
        _prep(rel_bias, norm1_gain[i], w_in[i], conv_w[i], conv_b[i], dt_bias_f[i], dt_bias_b[i],
              a_log_f[i], a_log_b[i], d_skip[i], ssd_norm_gain[i], w_ssd_proj[i], q_norm_gain[i],
              k_norm_gain[i], lambda_q1[i], lambda_k1[i], lambda_q2[i], lambda_k2[i], subln_gain[i],
              w_attn_proj[i], w_out[i], norm2_gain[i], peer_w_q[i], peer_sub_keys[i], peer_u[i],
              peer_v[i], lam_inits[i])
        for i in range(depth)
    ]
    outs = []
    for x in (x_prompt, x_sample):
        for p, lam_init in zip(preps, lam_inits):
            x = _layer(x, p, lam_init)
        outs.append(x)
    return tuple(outs)
```

```python
import functools
import math

import jax
import jax.numpy as jnp
from jax import lax
from jax.experimental import pallas as pl
from jax.experimental.pallas import tpu as pltpu

F32 = jnp.float32
BF16 = jnp.bfloat16

D_MODEL = 1024
SSD_D_INNER = 1024
SSD_HEAD_DIM = 64
SSD_N_HEADS = 16
SSD_N_GROUPS = 2
SSD_D_STATE = 64
SSD_CONV = 5
SSD_CHUNK = 128
SSD_CONV_DIM = 1280
ATTN_HEADS = 8
ATTN_HEAD_DIM = 64
ATTN_V_DIM = 128
REL_BUCKETS = 32
REL_MAX_DIST = 128
PEER_N_KEYS = 128
PEER_HEADS = 8
PEER_TOPK = 16
EPS = 1e-6

LANES = 128
SUBLANES = 8
VMEM_LIMIT = 56 * 1024 * 1024

COL_Z = 0
COL_Q = 1024
COL_K = 2048
COL_V = 3072
COL_GS = 4096
COL_GA = 5120
COL_XBC = 6400
PROJ_W = 7680
PROJ_TN = 1536

_CAND = [(a, b) for a in range(PEER_TOPK) for b in range(PEER_TOPK) if (a + 1) * (b + 1) <= PEER_TOPK]
_CAND_ROWS = 56


def _dot(a, b):
    return jnp.dot(a, b, preferred_element_type=F32)


def _dot_nt(a, b):
    return lax.dot_general(a, b, (((1,), (1,)), ((), ())), preferred_element_type=F32)


def _split3(v):
    hi = v.astype(BF16)
    r = v - hi.astype(F32)
    mid = r.astype(BF16)
    lo = (r - mid.astype(F32)).astype(BF16)
    return hi, mid, lo


def _exact_dot_rhs(v, m):
    hi, mid, lo = _split3(v)
    return _dot(hi, m) + _dot(mid, m) + _dot(lo, m)


def _expand_dot(v, m):
    hi = v.astype(BF16)
    lo = (v - hi.astype(F32)).astype(BF16)
    return _dot(hi, m) + _dot(lo, m)


def _exact_dot_lhs(m, v):
    hi, mid, lo = _split3(v)
    return _dot(m, hi) + _dot(m, mid) + _dot(m, lo)


def _cparams(sem, flags=None):
    return pltpu.CompilerParams(dimension_semantics=sem, vmem_limit_bytes=VMEM_LIMIT, flags=flags)


def _inproj_kernel(x_ref, g_ref, w_ref, wdt_ref, o_ref, dt_ref, xn_ref):
    j = pl.program_id(1)

    @pl.when(j == 0)
    def _():
        x = x_ref[...]
        ms = jnp.mean(x * x, axis=-1, keepdims=True)
        xn = (x * lax.rsqrt(ms + EPS) * g_ref[...]).astype(BF16)
        xn_ref[...] = xn
        dt_ref[...] = _dot(xn, wdt_ref[...])

    o_ref[...] = _dot(xn_ref[...], w_ref[...]).astype(BF16)


def _inproj(x2, gain, w_slab, w_dt, tm):
    T = x2.shape[0]
    grid = (T // tm, PROJ_W // PROJ_TN)
    return pl.pallas_call(
        _inproj_kernel,
        grid=grid,
        in_specs=[
            pl.BlockSpec((tm, D_MODEL), lambda i, j: (i, 0)),
            pl.BlockSpec((1, D_MODEL), lambda i, j: (0, 0)),
            pl.BlockSpec((D_MODEL, PROJ_TN), lambda i, j: (0, j)),
            pl.BlockSpec((D_MODEL, LANES), lambda i, j: (0, 0)),
        ],
        out_specs=[
            pl.BlockSpec((tm, PROJ_TN), lambda i, j: (i, j)),
            pl.BlockSpec((tm, LANES), lambda i, j: (i, 0)),
        ],
        out_shape=[
            jax.ShapeDtypeStruct((T, PROJ_W), BF16),
            jax.ShapeDtypeStruct((T, LANES), F32),
        ],
        scratch_shapes=[pltpu.VMEM((tm, D_MODEL), BF16)],
        compiler_params=_cparams(("parallel", "arbitrary")),
        name="inproj",
    )(x2, gain, w_slab, w_dt)


def _conv_kernel(xp_ref, xc_ref, xn_ref, cw_ref, cb_ref, o_ref, pad_ref, *, nc):
    c = pl.program_id(1)
    L = SSD_CHUNK
    H = SUBLANES
    xp = jnp.where(c > 0, xp_ref[...].astype(F32), 0.0)
    xn = jnp.where(c < nc - 1, xn_ref[...].astype(F32), 0.0)
    pad_ref[0:H, :] = xp
    pad_ref[H:H + L, :] = xc_ref[...].astype(F32)
    pad_ref[H + L:2 * H + L, :] = xn
    half = (SSD_CONV - 1) // 2
    acc = jnp.broadcast_to(cb_ref[...], (L, SSD_CONV_DIM))
    for k in range(SSD_CONV):
        off = H - half + k
        acc = acc + cw_ref[k:k + 1, :] * pad_ref[off:off + L, :]
    o_ref[...] = (acc * jax.nn.sigmoid(acc)).astype(BF16)


def _conv(proj3, cw, cb):
    B, S, _ = proj3.shape
    L = SSD_CHUNK
    nc = S // L
    rb = L // SUBLANES
    cblk = COL_XBC // SSD_CONV_DIM
    return pl.pallas_call(
        functools.partial(_conv_kernel, nc=nc),
        grid=(B, nc),
        in_specs=[
            pl.BlockSpec((None, SUBLANES, SSD_CONV_DIM), lambda b, c: (b, jnp.maximum(c * rb - 1, 0), cblk)),
            pl.BlockSpec((None, L, SSD_CONV_DIM), lambda b, c: (b, c, cblk)),
            pl.BlockSpec((None, SUBLANES, SSD_CONV_DIM),
                         lambda b, c: (b, jnp.minimum((c + 1) * rb, S // SUBLANES - 1), cblk)),
            pl.BlockSpec((SUBLANES, SSD_CONV_DIM), lambda b, c: (0, 0)),
            pl.BlockSpec((1, SSD_CONV_DIM), lambda b, c: (0, 0)),
        ],
        out_specs=pl.BlockSpec((None, L, SSD_CONV_DIM), lambda b, c: (b, c, 0)),
        out_shape=jax.ShapeDtypeStruct((B, S, SSD_CONV_DIM), BF16),
        scratch_shapes=[pltpu.VMEM((L + 2 * SUBLANES, SSD_CONV_DIM), F32)],
        compiler_params=_cparams(("parallel", "parallel")),
        name="ssd_conv",
    )(proj3, proj3, proj3, cw, cb)


def _ssd_sweep_kernel(xbc_ref, dt_ref, dtb_ref, alog_ref, ex_ref, y_ref, state_ref, *, rev, hoff):
    c = pl.program_id(1)
    L = SSD_CHUNK

    @pl.when(c == 0)
    def _():
        state_ref[...] = jnp.zeros_like(state_ref)

    xbc = xbc_ref[...]
    xs = xbc[:, :SSD_D_INNER].astype(F32)
    Bm = xbc[:, SSD_D_INNER:SSD_D_INNER + LANES]
    Cm = xbc[:, SSD_D_INNER + LANES:SSD_D_INNER + 2 * LANES]
    ex = ex_ref[...]

    dt = jax.nn.softplus(dt_ref[...] + dtb_ref[...])
    a = dt * (-jnp.exp(alog_ref[...]))
    row = lax.broadcasted_iota(jnp.int32, (L, L), 0)
    col = lax.broadcasted_iota(jnp.int32, (L, L), 1)
    tri = (col >= row) if rev else (col <= row)
    cs = _exact_dot_lhs(tri.astype(BF16), a)
    csT = cs.T
    end = 0 if rev else L - 1
    cs_end = cs[end:end + 1, :]

    dtT = dt.T
    xs_b = xbc[:, :SSD_D_INNER]
    xw = (xs * _expand_dot(dt * jnp.exp(cs_end - cs), ex)).astype(BF16)

    lane = lax.broadcasted_iota(jnp.int32, (1, LANES), 1)
    g0 = lane < SSD_D_STATE
    zero_b = jnp.zeros((), BF16)
    cb0 = _dot_nt(jnp.where(g0, Cm, zero_b), Bm)
    cb1 = _dot_nt(jnp.where(g0, zero_b, Cm), Bm)

    ys = []
    for pair in range(SSD_N_HEADS // 2):
        ms = []
        for hh in (2 * pair, 2 * pair + 1):
            k = hoff + hh
            seg = cs[:, k:k + 1] - csT[k:k + 1, :]
            dec = jnp.exp(jnp.where(tri, seg, -jnp.inf))
            cbg = cb0 if hh < SSD_N_HEADS // 2 else cb1
            ms.append((cbg * dec * dtT[k:k + 1, :]).astype(BF16))
        lhs = jnp.concatenate(ms, axis=1)
        xp = xs_b[:, LANES * pair:LANES * (pair + 1)]
        rhs = jnp.concatenate([jnp.where(g0, xp, zero_b), jnp.where(g0, zero_b, xp)], axis=0)
        ys.append(_dot(lhs, rhs))
    y_diag = jnp.concatenate(ys, axis=1)

    st = state_ref[...]
    y_off = _dot(Cm, st.astype(BF16)) * _expand_dot(jnp.exp(cs), ex)
    y_ref[...] = y_diag + y_off

    dec_c = _expand_dot(jnp.broadcast_to(jnp.exp(cs_end), (SUBLANES, LANES)), ex)[0:1, :]
    BmT = Bm.astype(F32).T.astype(BF16)
    upd = _dot(BmT, xw)
    r2 = lax.broadcasted_iota(jnp.int32, (LANES, SSD_D_INNER), 0) // SSD_D_STATE
    c2 = lax.broadcasted_iota(jnp.int32, (LANES, SSD_D_INNER), 1) // (SSD_D_INNER // SSD_N_GROUPS)
    state_ref[...] = st * dec_c + jnp.where(r2 == c2, upd, 0.0)


def _ssd_sweep(xbc3, dt3, dtb_row, alog_row, ex, rev):
    B, S, _ = xbc3.shape
    L = SSD_CHUNK
    nc = S // L
    hoff = SSD_N_HEADS if rev else 0
    if rev:
        cmap = lambda b, c: (b, nc - 1 - c, 0)
    else:
        cmap = lambda b, c: (b, c, 0)
    return pl.pallas_call(
        functools.partial(_ssd_sweep_kernel, rev=rev, hoff=hoff),
        grid=(B, nc),
        in_specs=[
            pl.BlockSpec((None, L, SSD_CONV_DIM), cmap),
            pl.BlockSpec((None, L, LANES), cmap),
            pl.BlockSpec((1, LANES), lambda b, c: (0, 0)),
            pl.BlockSpec((1, LANES), lambda b, c: (0, 0)),
            pl.BlockSpec((LANES, SSD_D_INNER), lambda b, c: (0, 0)),
        ],
        out_specs=pl.BlockSpec((None, L, SSD_D_INNER), cmap),
        out_shape=jax.ShapeDtypeStruct((B, S, SSD_D_INNER), F32),
        scratch_shapes=[pltpu.VMEM((LANES, SSD_D_INNER), F32)],
        compiler_params=_cparams(("parallel", "arbitrary")),
        name="ssd_sweep_rev" if rev else "ssd_sweep_fwd",
    )(xbc3, dt3, dtb_row, alog_row, ex)


ATTN_TQ_MAX = 512
ATTN_SCORE_BYTES = 16 * 1024 * 1024
ATTN_KB = LANES
ATTN_PV_TILES = 4
ATTN_QK_UNROLL_MAX = 30
LOG2E = math.log2(math.e)


def _attn_kernel(q_ref, k_ref, v_ref, bias_ref, cfar_ref, qg_ref, kg_ref, sg_ref, lam_ref, o_ref,
                 kn_ref, s1_ref, s2_ref, mx1_ref, mx2_ref, *, nk, scale, out_scale, qk_unroll):
    qi = pl.program_id(2)
    tq = q_ref.shape[0]
    kb = ATTN_KB
    n_near = tq // kb + 2
    r = lax.broadcasted_iota(jnp.int32, (LANES, LANES), 0) // ATTN_HEAD_DIM
    c = lax.broadcasted_iota(jnp.int32, (LANES, LANES), 1) // ATTN_HEAD_DIM
    ones2 = (r == c).astype(BF16)
    lane = lax.broadcasted_iota(jnp.int32, (1, LANES), 1)
    first = lane < ATTN_HEAD_DIM
    zero_b = jnp.zeros((), BF16)

    def halfnorm(x, g):
        ms = _exact_dot_rhs(x * x, ones2) * (1.0 / ATTN_HEAD_DIM)
        return x * lax.rsqrt(ms + EPS) * g

    @pl.when(qi == 0)
    def _():
        cl = cfar_ref[0:1, :]
        cr = cfar_ref[1:2, :]
        cl_hi = cl.astype(BF16).astype(F32)
        cr_hi = cr.astype(BF16).astype(F32)
        rowi = lax.broadcasted_iota(jnp.int32, (LANES, LANES), 0)
        top = rowi < ATTN_HEAD_DIM
        cpart = jnp.where(rowi == 0, cl_hi, jnp.where(rowi == 1, cl - cl_hi,
                          jnp.where(rowi == 2, cr_hi, jnp.where(rowi == 3, cr - cr_hi, 0.0)))).astype(BF16)
        cpart = jnp.concatenate([cpart, cpart], axis=1)

        def prep(t, carry):
            src = pl.multiple_of(t * kb, kb)
            knT = halfnorm(k_ref[pl.ds(src, kb), :].astype(F32), kg_ref[...]).T
            kpart = jnp.concatenate([jnp.where(top, knT, 0.0), jnp.where(top, 0.0, knT)], axis=1).astype(BF16)
            kn_ref[t] = jnp.concatenate([kpart, cpart], axis=0)
            return carry

        lax.fori_loop(0, nk, prep, 0, unroll=4)

    qn = (halfnorm(q_ref[...].astype(F32), qg_ref[...]) * (scale * LOG2E)).astype(BF16)
    sw_l = jnp.broadcast_to(jnp.where(lane < 2, 1.0, 0.0), (tq, LANES)).astype(BF16)
    sw_r = jnp.broadcast_to(jnp.where((lane >= 2) & (lane < 4), 1.0, 0.0), (tq, LANES)).astype(BF16)
    q_left = jnp.concatenate([qn, sw_l], axis=1)
    q_right = jnp.concatenate([qn, sw_r], axis=1)
    q_near = jnp.concatenate([qn, jnp.zeros((tq, LANES), BF16)], axis=1)

    def score_tile(t, qz, bias):
        s12 = _dot(qz, kn_ref[t])
        s1 = s12[:, :kb]
        s2 = s12[:, kb:]
        if bias is not None:
            s1 = s1 + bias
            s2 = s2 + bias
        s1_ref[t] = s1
        s2_ref[t] = s2
        return s1, s2

    near_lo = qi * (tq // kb) - 1
    n_left = jnp.maximum(near_lo, 0)
    near_hi = jnp.minimum(near_lo + n_near, nk)
    n_far = n_left + (nk - near_hi)

    def far_body(it, carry):
        m1, m2 = carry
        for u in range(qk_unroll):
            f = jnp.minimum(it * qk_unroll + u, n_far - 1)
            t = jnp.where(f < n_left, f, f - n_left + near_hi)
            s1, s2 = score_tile(t, jnp.where(f < n_left, q_left, q_right), None)
            m1 = jnp.maximum(m1, s1)
            m2 = jnp.maximum(m2, s2)
        return m1, m2

    minf = jnp.full((tq, LANES), -jnp.inf, F32)
    far_m1, far_m2 = lax.fori_loop(0, (n_far + qk_unroll - 1) // qk_unroll, far_body, (minf, minf))
    interior = (near_lo >= 0) & (near_lo + n_near <= nk)

    @pl.when(interior)
    def _():
        m1, m2 = far_m1, far_m2
        for j in range(n_near):
            s1, s2 = score_tile(near_lo + j, q_near, bias_ref[j])
            m1 = jnp.maximum(m1, s1)
            m2 = jnp.maximum(m2, s2)
        mx1_ref[...] = m1
        mx2_ref[...] = m2

    @pl.when(jnp.logical_not(interior))
    def _():
        mx1_ref[...] = far_m1
        mx2_ref[...] = far_m2
        for j in range(n_near):
            t = near_lo + j

            @pl.when((t >= 0) & (t < nk))
            def _():
                s1, s2 = score_tile(t, q_near, bias_ref[j])
                mx1_ref[...] = jnp.maximum(mx1_ref[...], s1)
                mx2_ref[...] = jnp.maximum(mx2_ref[...], s2)

    m1 = jnp.max(mx1_ref[...], axis=1, keepdims=True)
    m2 = jnp.max(mx2_ref[...], axis=1, keepdims=True)

    def pv_body(ch, carry):
        l1, l2, a1, a2 = carry
        p1s, p2s = [], []
        for u in range(ATTN_PV_TILES):
            t = ch * ATTN_PV_TILES + u
            p1 = jnp.exp2(s1_ref[t] - m1)
            p2 = jnp.exp2(s2_ref[t] - m2)
            l1 = l1 + p1
            l2 = l2 + p2
            p1s.append(p1.astype(BF16))
            p2s.append(p2.astype(BF16))
        vch = v_ref[pl.ds(pl.multiple_of(ch * ATTN_PV_TILES * kb, ATTN_PV_TILES * kb), ATTN_PV_TILES * kb), :]
        a1 = a1 + _dot(jnp.concatenate(p1s, axis=1), vch)
        a2 = a2 + _dot(jnp.concatenate(p2s, axis=1), vch)
        return l1, l2, a1, a2

    z = jnp.zeros((tq, LANES), F32)
    l1, l2, a1, a2 = lax.fori_loop(0, nk // ATTN_PV_TILES, pv_body, (z, z, z, z), unroll=8)
    l1 = jnp.sum(l1, axis=1, keepdims=True)
    l2 = jnp.sum(l2, axis=1, keepdims=True)
    o = a1 / l1 - lam_ref[...] * (a2 / l2)
    ms = jnp.mean(o * o, axis=-1, keepdims=True)
    o_ref[...] = (o * lax.rsqrt(ms + EPS) * sg_ref[...] * out_scale).astype(BF16)


def _attn_tq(S):
    tq = ATTN_TQ_MAX
    while 2 * 4 * S * tq > ATTN_SCORE_BYTES or S % tq:
        tq //= 2
    return tq


def _attention(proj3, rel_bias, qg, kg, sg, lam_row, lam_init):
    B, S, _ = proj3.shape
    tq = _attn_tq(S)
    bias_near, cfar = _bias_tiles(rel_bias, tq)
    nq = S // tq
    nk = S // ATTN_KB
    n_near = tq // ATTN_KB + 2
    n_far = max(nk - n_near, 1)
    qk_unroll = max(d for d in range(1, ATTN_QK_UNROLL_MAX + 1) if n_far % d == 0)
    return pl.pallas_call(
        functools.partial(_attn_kernel, nk=nk, scale=ATTN_HEAD_DIM ** -0.5, out_scale=1.0 - lam_init,
                          qk_unroll=qk_unroll),
        grid=(B, ATTN_HEADS, nq),
        in_specs=[
            pl.BlockSpec((None, tq, LANES), lambda b, h, q: (b, q, COL_Q // LANES + h)),
            pl.BlockSpec((None, S, LANES), lambda b, h, q: (b, 0, COL_K // LANES + h)),
            pl.BlockSpec((None, S, LANES), lambda b, h, q: (b, 0, COL_V // LANES + h)),
            pl.BlockSpec((None, n_near, tq, LANES), lambda b, h, q: (h, 0, 0, 0)),
            pl.BlockSpec((None, SUBLANES, LANES), lambda b, h, q: (h, 0, 0)),
            pl.BlockSpec((1, LANES), lambda b, h, q: (0, 0)),
            pl.BlockSpec((1, LANES), lambda b, h, q: (0, 0)),
            pl.BlockSpec((1, LANES), lambda b, h, q: (0, 0)),
            pl.BlockSpec((1, LANES), lambda b, h, q: (0, 0)),
        ],
        out_specs=pl.BlockSpec((None, tq, LANES), lambda b, h, q: (b, q, h)),
        out_shape=jax.ShapeDtypeStruct((B, S, ATTN_HEADS * ATTN_V_DIM), BF16),
        scratch_shapes=[
            pltpu.VMEM((nk, 2 * LANES, 2 * LANES), BF16),
            pltpu.VMEM((nk, tq, LANES), F32),
            pltpu.VMEM((nk, tq, LANES), F32),
            pltpu.VMEM((tq, LANES), F32),
            pltpu.VMEM((tq, LANES), F32),
        ],
        compiler_params=_cparams(("parallel", "parallel", "arbitrary")),
        name="diff_attn",
    )(proj3, proj3, proj3, bias_near, cfar, qg, kg, sg, lam_row)


def _mix_kernel(x_ref, yf_ref, yb_ref, xbc_ref, z_ref, at_ref, gs_ref, ga_ref, dsk_ref, ng_ref,
                wssd_ref, wattn_ref, wout_ref, h_ref):
    xs = xbc_ref[...][:, :SSD_D_INNER].astype(F32)
    y = yf_ref[...] + yb_ref[...] + xs * dsk_ref[...]
    z = z_ref[...].astype(F32)
    y = y * (z * jax.nn.sigmoid(z))
    gw = SSD_D_INNER // SSD_N_GROUPS
    parts = []
    for g in range(SSD_N_GROUPS):
        yg = y[:, g * gw:(g + 1) * gw]
        parts.append(yg * lax.rsqrt(jnp.mean(yg * yg, axis=-1, keepdims=True) + EPS))
    yn = (jnp.concatenate(parts, axis=1) * ng_ref[...]).astype(BF16)
    y_ssd = _dot(yn, wssd_ref[...])
    y_attn = _dot(at_ref[...], wattn_ref[...])
    mixed = (jax.nn.sigmoid(gs_ref[...].astype(F32)) * y_ssd
             + jax.nn.sigmoid(ga_ref[...].astype(F32)) * y_attn)
    h_ref[...] = x_ref[...] + _dot(mixed.astype(BF16), wout_ref[...])


def _mix(x2, yf2, yb2, xbc2, proj2, attn2, dsk_row, ng_row, wssd, wattn, wout, tm):
    T = x2.shape[0]
    row = lambda i: (i, 0)
    const = lambda i: (0, 0)
    return pl.pallas_call(
        _mix_kernel,
        grid=(T // tm,),
        in_specs=[
            pl.BlockSpec((tm, D_MODEL), row),
            pl.BlockSpec((tm, SSD_D_INNER), row),
            pl.BlockSpec((tm, SSD_D_INNER), row),
            pl.BlockSpec((tm, SSD_CONV_DIM), row),
            pl.BlockSpec((tm, D_MODEL), lambda i: (i, COL_Z // D_MODEL)),
            pl.BlockSpec((tm, D_MODEL), row),
            pl.BlockSpec((tm, D_MODEL), lambda i: (i, COL_GS // D_MODEL)),
            pl.BlockSpec((tm, D_MODEL), lambda i: (i, COL_GA // D_MODEL)),
            pl.BlockSpec((1, D_MODEL), const),
            pl.BlockSpec((1, D_MODEL), const),
            pl.BlockSpec((D_MODEL, D_MODEL), const),
            pl.BlockSpec((D_MODEL, D_MODEL), const),
            pl.BlockSpec((D_MODEL, D_MODEL), const),
        ],
        out_specs=pl.BlockSpec((tm, D_MODEL), row),
        out_shape=jax.ShapeDtypeStruct((T, D_MODEL), F32),
        compiler_params=_cparams(("parallel",)),
        name="mix",
    )(x2, yf2, yb2, xbc2, proj2, attn2, proj2, proj2, dsk_row, ng_row, wssd, wattn, wout)


def _extract_topk(work, n):
    iota = lax.broadcasted_iota(jnp.int32, work.shape, 0)
    big = jnp.int32(work.shape[0])
    rank = jnp.full(work.shape, n, jnp.int32)
    vals = []
    for r in range(n):
        m = jnp.max(work, axis=0, keepdims=True)
        sel = iota == jnp.min(jnp.where(work == m, iota, big), axis=0, keepdims=True)
        rank = jnp.where(sel, r, rank)
        work = jnp.where(sel, -jnp.inf, work)
        vals.append(m)
    return rank, vals


def _topk_values(s, n):
    m = jnp.max(s, axis=0, keepdims=True)
    vals = [m]
    for _ in range(1, n):
        m = jnp.max(jnp.where(s < m, s, -jnp.inf), axis=0, keepdims=True)
        vals.append(m)
    return vals


_ROUTE_ROWS = 16
_DENSE_ROWS = 32


def _count(mask):
    return jnp.sum(jnp.where(mask, 1.0, 0.0), axis=0, keepdims=True)


def _fill_candidates(cand_ref, top0, top1):
    for r, (a, b) in enumerate(_CAND):
        cand_ref[r:r + 1, :] = top0[a] + top1[b]
    return cand_ref[...]


def _route_exact(s0, s1, cand_ref):
    K = PEER_TOPK
    rank0, top0 = _extract_topk(s0, K)
    rank1, top1 = _extract_topk(s1, K)
    cand = _fill_candidates(cand_ref, top0, top1)
    crank, cvals = _extract_topk(cand, K)
    csel = crank < K
    zsum = jnp.sum(jnp.where(csel, jnp.exp(cand - cvals[0]), 0.0), axis=0, keepdims=True)
    crow = lax.broadcasted_iota(jnp.int32, cand.shape, 0)
    n_i = jnp.zeros(s0.shape, F32)
    r0 = 0
    for a in range(K):
        r1 = r0 + K // (a + 1)
        n_a = _count(csel & (crow >= r0) & (crow < r1))
        n_i = n_i + jnp.where(rank0 == a, n_a, 0.0)
        r0 = r1
    f_i = jnp.where(rank0 < K, jnp.exp(s0 - top0[0]), 0.0)
    e_b = jnp.where(rank1 < K, jnp.exp(s1 - top1[0]) / zsum, 0.0)
    return n_i, f_i, rank1.astype(F32), e_b


def _route_fast(s0, s1, cand_ref):
    K = PEER_TOPK
    top0 = _topk_values(s0, K)
    top1 = _topk_values(s1, K)
    cand = _fill_candidates(cand_ref, top0, top1)
    ctop = _topk_values(cand, K)
    tau = ctop[K - 1]
    in0 = s0 >= top0[K - 1]
    in1 = s1 >= top1[K - 1]
    n_parts, rank_parts = [], []
    for lo in range(0, PEER_N_KEYS, _ROUTE_ROWS):
        rows0 = s0[lo:lo + _ROUTE_ROWS]
        rows1 = s1[lo:lo + _ROUTE_ROWS]
        n_p = jnp.zeros(rows0.shape, F32)
        r_p = jnp.zeros(rows1.shape, F32)
        for r in range(K):
            n_p = n_p + jnp.where(rows0 + top1[r] >= tau, 1.0, 0.0)
            r_p = r_p + jnp.where(rows1 < top1[r], 1.0, 0.0)
        n_parts.append(n_p)
        rank_parts.append(r_p)
    n_i = jnp.where(in0, jnp.concatenate(n_parts, axis=0), 0.0)
    rank1 = jnp.concatenate(rank_parts, axis=0)
    zsum = jnp.sum(jnp.where(cand >= tau, jnp.exp(cand - ctop[0]), 0.0), axis=0, keepdims=True)
    f_i = jnp.where(in0, jnp.exp(s0 - top0[0]), 0.0)
    e_b = jnp.where(in1, jnp.exp(s1 - top1[0]) / zsum, 0.0)
    bad = (jnp.abs(_count(in0) - K) + jnp.abs(_count(in1) - K)
           + jnp.abs(jnp.sum(n_i, axis=0, keepdims=True) - K))
    return (n_i, f_i, rank1, e_b), jnp.max(bad) == 0.0


def _peer_kernel(h_ref, g2_ref, wqT_ref, sk_ref, u_ref, vT_ref, o_ref,
                 hnT_ref, qT_ref, rb_ref, eb_ref, ni_ref, fi_ref, acc_ref, cand_ref, hid_a_ref, hid_b_ref):
    i = pl.program_id(1)
    K = PEER_TOPK
    tb = h_ref.shape[0]
    gw = qT_ref.shape[2]
    n_groups = tb // gw
    half = PEER_N_KEYS // 2

    @pl.when(i == 0)
    def _router():
        h = h_ref[...]
        hn = h * lax.rsqrt(jnp.mean(h * h, axis=-1, keepdims=True) + EPS) * g2_ref[...]
        hnT = hn.T.astype(BF16)
        hnT_ref[...] = hnT
        qT = _dot(wqT_ref[...], hnT).astype(BF16)
        for g in range(n_groups):
            qT_ref[g] = qT[:, g * gw:(g + 1) * gw]
        acc_ref[...] = jnp.zeros_like(acc_ref)
        hid_b_ref[...] = jnp.zeros_like(hid_b_ref)
        cand_ref[...] = jnp.full(cand_ref.shape, -jnp.inf, F32)

        def route(x, carry):
            hd = x // n_groups
            g = x % n_groups
            scores = []
            for m in range(2):
                off = pl.multiple_of(hd * PEER_N_KEYS + m * half, half)
                scores.append(_dot(sk_ref[m], qT_ref[g, pl.ds(off, half), :]))
            fast, ok = _route_fast(scores[0], scores[1], cand_ref)
            n_i, f_i, rank1, e_b = lax.cond(ok, lambda: fast, lambda: _route_exact(scores[0], scores[1], cand_ref))
            ni_ref[hd, g] = n_i
            fi_ref[hd, g] = f_i
            rb_ref[hd, g] = rank1.astype(BF16)
            eb_ref[hd, g] = e_b.astype(BF16)
            return carry

        lax.fori_loop(0, PEER_HEADS * n_groups, route, 0)

    def lanes(ref, hd, rows):
        return jnp.concatenate([ref[hd, g, rows, :] for g in range(n_groups)], axis=1)

    def dense_step(read_ref, write_ref):
        hid_new = _dot(u_ref[...], hnT_ref[...])
        prev = jnp.maximum(i - 1, 0)
        ws = []
        for it in range(PEER_STEP_TILES):
            key0 = prev * PEER_STEP_TILES + it
            n_rows = [lanes(ni_ref, hd, pl.ds(key0, 1)).astype(BF16) for hd in range(PEER_HEADS)]
            f_rows = [lanes(fi_ref, hd, pl.ds(key0, 1)).astype(BF16) for hd in range(PEER_HEADS)]
            for lo in range(0, PEER_N_KEYS, _DENSE_ROWS):
                rows = slice(lo, lo + _DENSE_ROWS)
                hid = read_ref[it * PEER_N_KEYS + lo:it * PEER_N_KEYS + lo + _DENSE_ROWS, :]
                hb = hid.astype(BF16)
                ge = hb * (lax.erf(hb) + jnp.ones((), BF16))
                gate = jnp.zeros(hid.shape, BF16)
                for hd in range(PEER_HEADS):
                    sel = lanes(rb_ref, hd, rows) < n_rows[hd]
                    gate = gate + jnp.where(sel, lanes(eb_ref, hd, rows) * f_rows[hd], jnp.zeros((), BF16))
                ws.append(ge * gate)
        kw = 2 * PEER_N_KEYS
        per = kw // _DENSE_ROWS
        upd = _dot(vT_ref[:, :kw], jnp.concatenate(ws[:per], axis=0))
        for p in range(1, PEER_STEP_TILES // 2):
            upd = upd + _dot(vT_ref[:, p * kw:(p + 1) * kw], jnp.concatenate(ws[p * per:(p + 1) * per], axis=0))
        acc_ref[...] += upd
        write_ref[...] = hid_new

    @pl.when(i % 2 == 0)
    def _():
        dense_step(hid_b_ref, hid_a_ref)

    @pl.when(i % 2 == 1)
    def _():
        dense_step(hid_a_ref, hid_b_ref)

    @pl.when(i == pl.num_programs(1) - 1)
    def _():
        o_ref[...] = h_ref[...] + acc_ref[...].T


PEER_STEP_TILES = 8
PEER_GROUP = 512


def _peer(h2, g2_row, wqT, sk, u_b, vT_b, tb):
    T = h2.shape[0]
    nk = PEER_N_KEYS
    ne = nk * PEER_STEP_TILES
    n_blocks = nk // PEER_STEP_TILES
    gw = min(tb, PEER_GROUP)
    tile = (PEER_HEADS, tb // gw, nk, gw)
    return pl.pallas_call(
        _peer_kernel,
        grid=(T // tb, n_blocks + 1),
        in_specs=[
            pl.BlockSpec((tb, D_MODEL), lambda t, i: (t, 0)),
            pl.BlockSpec((1, D_MODEL), lambda t, i: (0, 0)),
            pl.BlockSpec((D_MODEL, D_MODEL), lambda t, i: (0, 0)),
            pl.BlockSpec((2, nk, nk // 2), lambda t, i: (0, 0, 0)),
            pl.BlockSpec((ne, D_MODEL), lambda t, i: (jnp.minimum(i, n_blocks - 1), 0)),
            pl.BlockSpec((D_MODEL, ne), lambda t, i: (0, jnp.maximum(i - 1, 0))),
        ],
        out_specs=pl.BlockSpec((tb, D_MODEL), lambda t, i: (t, 0)),
        out_shape=jax.ShapeDtypeStruct((T, D_MODEL), F32),
        scratch_shapes=[
            pltpu.VMEM((D_MODEL, tb), BF16),
            pltpu.VMEM((tb // gw, D_MODEL, gw), BF16),
            pltpu.VMEM(tile, BF16),
            pltpu.VMEM(tile, BF16),
            pltpu.VMEM(tile, F32),
            pltpu.VMEM(tile, F32),
            pltpu.VMEM((D_MODEL, tb), F32),
            pltpu.VMEM((_CAND_ROWS, gw), F32),
            pltpu.VMEM((ne, tb), F32),
            pltpu.VMEM((ne, tb), F32),
        ],
        compiler_params=_cparams(("parallel", "arbitrary")),
        name="peer",
    )(h2, g2_row, wqT, sk, u_b, vT_b)


def _rel_bucket(rel):
    nb = REL_BUCKETS // 2
    ret = jnp.where(rel > 0, nb, 0)
    n = jnp.abs(rel)
    max_exact = nb // 2
    nf = jnp.maximum(n, 1).astype(F32)
    large = max_exact + (jnp.log(nf / max_exact) / math.log(REL_MAX_DIST / max_exact)
                         * (nb - max_exact)).astype(jnp.int32)
    large = jnp.minimum(large, nb - 1)
    return ret + jnp.where(n < max_exact, n, large)


def _bias_tiles(rel_bias, tq):
    kb = ATTN_KB
    n_near = tq // kb + 2
    width = n_near * kb
    rel = jnp.arange(width + tq - 1, dtype=jnp.int32) - (tq - 1) - kb
    table = (rel_bias[_rel_bucket(rel)].astype(F32) * LOG2E).T
    rows = jnp.stack([table[:, tq - 1 - r:tq - 1 - r + width] for r in range(tq)], axis=1)
    near = rows.reshape(ATTN_HEADS, tq, n_near, kb).transpose(0, 2, 1, 3)
    far = rel_bias[_rel_bucket(jnp.array([-REL_MAX_DIST, REL_MAX_DIST], jnp.int32))].astype(F32).T * LOG2E
    cfar = jnp.broadcast_to(jnp.pad(far, ((0, 0), (0, SUBLANES - 2)))[:, :, None], (ATTN_HEADS, SUBLANES, LANES))
    return near, cfar


def _pad_rows(a, n):
    return jnp.pad(a, ((0, n - a.shape[0]), (0, 0)))


def _prep(rel_bias, norm1_gain, w_in, conv_w, conv_b, dt_bias_f, dt_bias_b, a_log_f, a_log_b, d_skip,
          ssd_norm_gain, w_ssd_proj, q_norm_gain, k_norm_gain, lambda_q1, lambda_k1, lambda_q2, lambda_k2,
          subln_gain, w_attn_proj, w_out, norm2_gain, peer_w_q, peer_sub_keys, peer_u, peer_v, lam_init):
    o1 = SSD_D_INNER
    o2 = o1 + SSD_CONV_DIM
    o3 = o2 + SSD_N_HEADS
    o4 = o3 + SSD_N_HEADS
    o5 = o4 + 1024
    o6 = o5 + 1024
    o7 = o6 + 1024
    o8 = o7 + D_MODEL
    w = w_in
    zeros = jnp.zeros((D_MODEL, COL_XBC - (COL_GA + D_MODEL)), F32)
    slab = jnp.concatenate([w[:, :o1], w[:, o4:o5], w[:, o5:o6], w[:, o6:o7], w[:, o7:o8], w[:, o8:],
                            zeros, w[:, o1:o2]], axis=1).astype(BF16)
    w_dt = jnp.pad(w[:, o2:o4], ((0, 0), (0, LANES - 2 * SSD_N_HEADS))).astype(BF16)
    row128 = lambda f, b: jnp.pad(jnp.concatenate([f, b]), (0, LANES - 2 * SSD_N_HEADS)).reshape(1, LANES)
    hp = jnp.arange(SSD_D_INNER, dtype=jnp.int32) // SSD_HEAD_DIM
    rows = jnp.arange(LANES, dtype=jnp.int32)[:, None]
    f32 = F32
    lam = (jnp.exp(jnp.sum(lambda_q1.astype(f32) * lambda_k1.astype(f32)))
           - jnp.exp(jnp.sum(lambda_q2.astype(f32) * lambda_k2.astype(f32))) + lam_init)
    return dict(
        gain1=norm1_gain.reshape(1, D_MODEL),
        slab=slab,
        w_dt=w_dt,
        cw=_pad_rows(conv_w.reshape(SSD_CONV, SSD_CONV_DIM), SUBLANES),
        cb=conv_b.reshape(1, SSD_CONV_DIM),
        dtb=row128(dt_bias_f, dt_bias_b),
        alog=row128(a_log_f, a_log_b),
        ex_f=(rows == hp[None, :]).astype(BF16),
        ex_b=(rows == hp[None, :] + SSD_N_HEADS).astype(BF16),
        dsk=jnp.repeat(d_skip, SSD_HEAD_DIM).reshape(1, SSD_D_INNER),
        ng=ssd_norm_gain.reshape(1, SSD_D_INNER),
        wssd=w_ssd_proj.astype(BF16),
        wattn=w_attn_proj.astype(BF16),
        wout=w_out.astype(BF16),
        rel_bias=rel_bias,
        qg=jnp.tile(q_norm_gain, 2).reshape(1, LANES),
        kg=jnp.tile(k_norm_gain, 2).reshape(1, LANES),
        sg=subln_gain.reshape(1, LANES),
        lam=jnp.broadcast_to(lam, (1, LANES)).astype(F32),
        g2=norm2_gain.reshape(1, D_MODEL),
        wqT=peer_w_q.T.astype(BF16),
        sk=peer_sub_keys.astype(BF16),
        u=(peer_u * (2.0 ** -0.5)).astype(BF16),
        vT=(peer_v.T * (2.0 ** -0.5)).astype(BF16),
    )


def _largest_divisor(n, cap):
    t = cap
    while n % t:
        t //= 2
    return t


def _layer(x, p, lam_init):
    B, S, _ = x.shape
    T = B * S
    x2 = x.reshape(T, D_MODEL)
    tm = _largest_divisor(T, 1024)
    proj2, dt2 = _inproj(x2, p["gain1"], p["slab"], p["w_dt"], tm)
    proj3 = proj2.reshape(B, S, PROJ_W)
    dt3 = dt2.reshape(B, S, LANES)
    xbc3 = _conv(proj3, p["cw"], p["cb"])
    yf = _ssd_sweep(xbc3, dt3, p["dtb"], p["alog"], p["ex_f"], rev=False)
    yb = _ssd_sweep(xbc3, dt3, p["dtb"], p["alog"], p["ex_b"], rev=True)
    attn = _attention(proj3, p["rel_bias"], p["qg"], p["kg"], p["sg"], p["lam"], lam_init)
    h2 = _mix(x2, yf.reshape(T, -1), yb.reshape(T, -1), xbc3.reshape(T, -1), proj2, attn.reshape(T, -1),
              p["dsk"], p["ng"], p["wssd"], p["wattn"], p["wout"], _largest_divisor(T, 512))
    y2 = _peer(h2, p["g2"], p["wqT"], p["sk"], p["u"], p["vT"], _largest_divisor(T, 512))
    return y2.reshape(B, S, D_MODEL)


def kernel(x_prompt, x_sample, rel_bias, norm1_gain, w_in, conv_w, conv_b, dt_bias_f, dt_bias_b, a_log_f, a_log_b, d_skip, ssd_norm_gain, w_ssd_proj, q_norm_gain, k_norm_gain, lambda_q1, lambda_k1, lambda_q2, lambda_k2, subln_gain, w_attn_proj, w_out, norm2_gain, peer_w_q, peer_sub_keys, peer_u, peer_v):
    depth = norm1_gain.shape[0]
    lam_inits = [0.8 - 0.6 * math.exp(-0.3 * i) for i in range(depth)]
    preps = [
        _prep(rel_bias, norm1_gain[i], w_in[i], conv_w[i], conv_b[i], dt_bias_f[i], dt_bias_b[i],
              a_log_f[i], a_log_b[i], d_skip[i], ssd_norm_gain[i], w_ssd_proj[i], q_norm_gain[i],
              k_norm_gain[i], lambda_q1[i], lambda_k1[i], lambda_q2[i], lambda_k2[i], subln_gain[i],
              w_attn_proj[i], w_out[i], norm2_gain[i], peer_w_q[i], peer_sub_keys[i], peer_u[i],
              peer_v[i], lam_inits[i])
        for i in range(depth)
    ]
    outs = []
    for x in (x_prompt, x_sample):
        for p, lam_init in zip(preps, lam_inits):
            x = _layer(x, p, lam_init)
        outs.append(x)
    return tuple(outs)
```
